```python
import jax, jax.numpy as jnp
from jax import lax
import numpy as np

D_MODEL = 2048
BATCH = 2
SEQ = 4096
DEPTH = 4
DEC_BATCH = 1
DEC_SEQ = 8192
PAST_LEN = 128

GRID_W = 64
EPS = 1e-6
MIN_FORGET = 1e-6
HG_HEADS = D_MODEL // 256
HG_DK = 128
HG_DV = 128
HG_QK = HG_HEADS * HG_DK
HG_WIDTH = HG_HEADS * HG_DV
HG_CHUNK = 64
ATT_HEADS = D_MODEL // 256
ATT_KV_HEADS = ATT_HEADS // 4
ATT_GROUP = ATT_HEADS // ATT_KV_HEADS
HEAD_DIM = 128
ATT_WIDTH = ATT_HEADS * HEAD_DIM
KV_WIDTH = ATT_KV_HEADS * HEAD_DIM
ROPE_FREQS = HEAD_DIM // 4
ROPE_THETA = 10000.0
Q_BLOCK = 128
MIX_WIDTH = HG_WIDTH + ATT_WIDTH
IN_SIZES = (HG_QK, HG_QK, HG_QK, HG_WIDTH, HG_WIDTH, ATT_WIDTH, KV_WIDTH, KV_WIDTH)
IN_COLS = HG_QK * 3 + HG_WIDTH * 2 + ATT_WIDTH + KV_WIDTH * 2
FFN_DIM = 5632
N_EXPERTS = 8
TOP_K = 2
EXPERT_DIM = 7168
MOE_BLOCK = 128
N_DENSE = (DEPTH + 1) // 2
N_MOE = DEPTH // 2

kernel_name = 'hymba_hgrn2_gqa2drope_moe_encoder'


def rmsnorm(x, g):
    xf = x.astype(jnp.float32)
    y = xf * lax.rsqrt(jnp.mean(xf * xf, axis=-1, keepdims=True) + EPS)
    return (y * g.astype(jnp.float32)).astype(x.dtype)


def hgrn_lower_bounds(lb_logits):
    s = jax.nn.softmax(lb_logits.astype(jnp.float32), axis=1)
    return jnp.cumsum(s, axis=1) - s[:, 0:1]


def gla_chunk_scan(q, k, logf, v):
    b, l, h, dk = q.shape
    dv = v.shape[-1]
    n = l // HG_CHUNK

    def to_chunks(t):
        return t.reshape(b, n, HG_CHUNK, h, t.shape[-1]).transpose(1, 0, 3, 2, 4)

    incl = jnp.tril(jnp.ones((HG_CHUNK, HG_CHUNK), bool))[:, :, None]

    def step(state, inp):
        qc, kc, fc, vc = inp
        a = jnp.cumsum(fc, axis=2)
        a_last = a[:, :, -1:, :]
        diff = a[:, :, :, None, :] - a[:, :, None, :, :]
        decay = jnp.where(incl, jnp.exp(jnp.minimum(diff, 0.0)), 0.0)
        scores = jnp.einsum('bhtsd,bhsd->bhts', qc[:, :, :, None, :] * decay, kc)
        o = (jnp.einsum('bhts,bhsv->bhtv', scores, vc)
             + jnp.einsum('bhtd,bhdv->bhtv', qc * jnp.exp(a), state))
        state = (jnp.exp(a_last[:, :, 0, :])[..., None] * state
                 + jnp.einsum('bhsd,bhsv->bhdv', kc * jnp.exp(a_last - a), vc))
        return state, o

    s0 = jnp.zeros((b, h, dk, dv), jnp.float32)
    _, o = lax.scan(step, s0, (to_chunks(q), to_chunks(k), to_chunks(logf), to_chunks(v)))
    return o.transpose(1, 0, 3, 2, 4).reshape(b, l, h, dv)


def hgrn2_group(q, zf, zb, i, g, lb_f, lb_b, out_gain, dtype):
    b, l, _ = q.shape
    qh = q.astype(jnp.float32).reshape(b, l, HG_HEADS, HG_DK)
    ih = i.astype(jnp.float32).reshape(b, l, HG_HEADS, HG_DV)

    def forget(z, lb):
        zf32 = z.astype(jnp.float32)
        f = lb + (1.0 - lb) * jax.nn.sigmoid(zf32)
        logf = jnp.log(jnp.maximum(f, MIN_FORGET)).reshape(b, l, HG_HEADS, HG_DK)
        kk = ((1.0 - lb) * jax.nn.sigmoid(-zf32)).reshape(b, l, HG_HEADS, HG_DK)
        return logf, kk

    lf, kf = forget(zf, lb_f)
    lr, kr = forget(zb, lb_b)
    o_fwd = gla_chunk_scan(qh, kf, lf, ih)
    flip = lambda t: t[:, ::-1]
    o_bwd = flip(gla_chunk_scan(flip(qh), flip(kr), flip(lr), flip(ih)))
    o = rmsnorm(o_fwd + o_bwd, out_gain).reshape(b, l, HG_WIDTH)
    return (o * jax.nn.silu(g.astype(jnp.float32))).astype(dtype)


def rope_tables(l):
    rows = l // GRID_W
    row = jnp.repeat(jnp.arange(rows), GRID_W)
    col = jnp.tile(jnp.arange(GRID_W), rows)
    pos = jnp.stack([row, col], axis=-1).astype(jnp.float32)
    inv = ROPE_THETA ** (-jnp.arange(ROPE_FREQS, dtype=jnp.float32) / ROPE_FREQS)
    ang = pos[:, :, None] * inv
    return jnp.cos(ang), jnp.sin(ang)


def apply_rope(x, cos, sin):
    b, l, h, _ = x.shape
    xr = x.reshape(b, l, h, 2, 2, ROPE_FREQS)
    x1, x2 = xr[..., 0, :], xr[..., 1, :]
    c, s = cos[None, :, None], sin[None, :, None]
    return jnp.stack([x1 * c - x2 * s, x2 * c + x1 * s], axis=-2).reshape(b, l, h, HEAD_DIM)


def attention_group(q, k, v, q_gain, k_gain, cos, sin, dtype):
    b, l, _ = q.shape
    qh = rmsnorm(q.reshape(b, l, ATT_HEADS, HEAD_DIM).astype(jnp.float32), q_gain)
    kh = rmsnorm(k.reshape(b, l, ATT_KV_HEADS, HEAD_DIM).astype(jnp.float32), k_gain)
    qh = apply_rope(qh, cos, sin) * (HEAD_DIM ** -0.5)
    kh = apply_rope(kh, cos, sin)
    vh = v.reshape(b, l, ATT_KV_HEADS, HEAD_DIM).astype(jnp.float32)
    nb = l // Q_BLOCK
    qb = qh.reshape(b, nb, Q_BLOCK, ATT_KV_HEADS, ATT_GROUP, HEAD_DIM).transpose(1, 0, 3, 4, 2, 5)
    kt = kh.transpose(0, 2, 1, 3)
    vt = vh.transpose(0, 2, 1, 3)

    def block(qblk):
        s = jnp.einsum('bkgqd,bksd->bkgqs', qblk, kt)
        p = jax.nn.softmax(s, axis=-1)
        return jnp.einsum('bkgqs,bksd->bkgqd', p, vt)

    o = lax.map(block, qb)
    return o.transpose(1, 0, 4, 2, 3, 5).reshape(b, l, ATT_WIDTH).astype(dtype)


def swiglu(h, wg, wu, wd):
    return (jax.nn.silu(h @ wg) * (h @ wu)) @ wd


def moe_swiglu(h, w_router, w_gate, w_up, w_down):
    b, l, d = h.shape
    n = b * l
    xt = h.reshape(n, d)
    logits = (xt @ w_router).astype(jnp.float32)
    top_val, top_idx = lax.top_k(logits, TOP_K)
    gates = jax.nn.softmax(top_val, axis=-1)
    flat_e = top_idx.reshape(-1)
    flat_tok = jnp.repeat(jnp.arange(n, dtype=jnp.int32), TOP_K)
    flat_w = gates.reshape(-1)
    order = jnp.argsort(flat_e)
    sorted_e = flat_e[order]
    counts = jnp.bincount(flat_e, length=N_EXPERTS)
    padded = (counts + MOE_BLOCK - 1) // MOE_BLOCK * MOE_BLOCK
    pad_end = jnp.cumsum(padded)
    pad_start = pad_end - padded
    grp_start = jnp.cumsum(counts) - counts
    rank = jnp.arange(n * TOP_K) - grp_start[sorted_e]
    dest = pad_start[sorted_e] + rank
    cap = n * TOP_K + N_EXPERTS * MOE_BLOCK
    n_blocks = cap // MOE_BLOCK
    buf_tok = jnp.full((cap,), n, jnp.int32).at[dest].set(flat_tok[order])
    buf_w = jnp.zeros((cap,), jnp.float32).at[dest].set(flat_w[order])
    block_e = jnp.minimum(jnp.searchsorted(pad_end, jnp.arange(n_blocks) * MOE_BLOCK, side='right'),
                          N_EXPERTS - 1)
    x_pad = jnp.concatenate([xt, jnp.zeros((1, d), xt.dtype)], axis=0)
    xb = x_pad[buf_tok].reshape(n_blocks, MOE_BLOCK, d)

    def expert_block(args):
        xblk, e = args
        return swiglu(xblk, w_gate[e], w_up[e], w_down[e])

    yb = lax.map(expert_block, (xb, block_e)).reshape(cap, d)
    y = jnp.zeros((n + 1, d), jnp.float32).at[buf_tok].add(yb.astype(jnp.float32) * buf_w[:, None])
    return y[:n].reshape(b, l, d).astype(h.dtype)


def run_trunk(x, c, lbs, w_mod, b_mod, norm_mix, norm_ffn, w_in, hg_out_norm, qk_norm, w_out,
              ffn_gate, ffn_up, ffn_down, router, exp_gate, exp_up, exp_down, final_norm):
    cos, sin = rope_tables(x.shape[1])
    splits = np.cumsum(IN_SIZES)[:-1].tolist()
    for layer in range(DEPTH):
        mod = jax.nn.silu(c) @ w_mod[layer] + b_mod[layer]
        sh1, sc1, g1, sh2, sc2, g2 = jnp.split(mod[:, None, :], 6, axis=-1)
        h = rmsnorm(x, norm_mix[layer]) * (1 + sc1) + sh1
        hq, hzf, hzb, hi, hg, aq, ak, av = jnp.split(h @ w_in[layer], splits, axis=-1)
        o_hg = hgrn2_group(hq, hzf, hzb, hi, hg, lbs[0, layer], lbs[1, layer], hg_out_norm[layer], x.dtype)
        o_att = attention_group(aq, ak, av, qk_norm[layer, 0], qk_norm[layer, 1], cos, sin, x.dtype)
        x = x + g1 * (jnp.concatenate([o_hg, o_att], axis=-1) @ w_out[layer])
        h = rmsnorm(x, norm_ffn[layer]) * (1 + sc2) + sh2
        j = layer // 2
        if layer % 2 == 0:
            f = swiglu(h, ffn_gate[j], ffn_up[j], ffn_down[j])
        else:
            f = moe_swiglu(h, router[j], exp_gate[j], exp_up[j], exp_down[j])
        x = x + g2 * f
    return rmsnorm(x, final_norm)


def setup_inputs(seed: int = 0) -> dict:
    key = jax.random.key(seed)
    ks = jax.random.split(key, 22)
    nrm = lambda k, shape, scale: jax.random.normal(k, shape, jnp.float32) * scale
    return {
        'x_prompt': nrm(ks[0], (BATCH, SEQ, D_MODEL), 1.0),
        'x_sample': nrm(ks[1], (DEC_BATCH, DEC_SEQ, D_MODEL), 1.0),
        'c_prompt': nrm(ks[2], (BATCH, D_MODEL), 1.0),
        'c_sample': nrm(ks[3], (DEC_BATCH, D_MODEL), 1.0),
        'w_mod': nrm(ks[4], (DEPTH, D_MODEL, 6 * D_MODEL), 0.5 * D_MODEL ** -0.5),
        'b_mod': nrm(ks[5], (DEPTH, 6 * D_MODEL), 0.01),
        'norm_mix': 1.0 + nrm(ks[6], (DEPTH, D_MODEL), 0.02),
        'norm_ffn': 1.0 + nrm(ks[7], (DEPTH, D_MODEL), 0.02),
        'w_in': nrm(ks[8], (DEPTH, D_MODEL, IN_COLS), D_MODEL ** -0.5),
        'hg_lb_logits': nrm(ks[9], (2, DEPTH, HG_QK), 0.5),
        'hg_out_norm': 1.0 + nrm(ks[10], (DEPTH, HG_DV), 0.02),
        'qk_norm': 1.0 + nrm(ks[11], (DEPTH, 2, HEAD_DIM), 0.02),
        'w_out': nrm(ks[12], (DEPTH, MIX_WIDTH, D_MODEL), MIX_WIDTH ** -0.5),
        'ffn_gate': nrm(ks[13], (N_DENSE, D_MODEL, FFN_DIM), D_MODEL ** -0.5),
        'ffn_up': nrm(ks[14], (N_DENSE, D_MODEL, FFN_DIM), D_MODEL ** -0.5),
        'ffn_down': nrm(ks[15], (N_DENSE, FFN_DIM, D_MODEL), FFN_DIM ** -0.5),
        'router': nrm(ks[16], (N_MOE, D_MODEL, N_EXPERTS), D_MODEL ** -0.5),
        'exp_gate': nrm(ks[17], (N_MOE, N_EXPERTS, D_MODEL, EXPERT_DIM), D_MODEL ** -0.5),
        'exp_up': nrm(ks[18], (N_MOE, N_EXPERTS, D_MODEL, EXPERT_DIM), D_MODEL ** -0.5),
        'exp_down': nrm(ks[19], (N_MOE, N_EXPERTS, EXPERT_DIM, D_MODEL), EXPERT_DIM ** -0.5),
        'final_norm': 1.0 + nrm(ks[20], (D_MODEL,), 0.02),
    }


def reference(x_prompt, x_sample, c_prompt, c_sample, w_mod, b_mod, norm_mix, norm_ffn, w_in,
              hg_lb_logits, hg_out_norm, qk_norm, w_out, ffn_gate, ffn_up, ffn_down, router,
              exp_gate, exp_up, exp_down, final_norm):
    lbs = hgrn_lower_bounds(hg_lb_logits)
    y_prompt = run_trunk(x_prompt, c_prompt, lbs, w_mod, b_mod, norm_mix, norm_ffn, w_in, hg_out_norm,
                         qk_norm, w_out, ffn_gate, ffn_up, ffn_down, router, exp_gate, exp_up, exp_down,
                         final_norm)
    y_sample = run_trunk(x_sample, c_sample, lbs, w_mod, b_mod, norm_mix, norm_ffn, w_in, hg_out_norm,
                         qk_norm, w_out, ffn_gate, ffn_up, ffn_down, router, exp_gate, exp_up, exp_down,
                         final_norm)
    return (y_prompt, y_sample)
```

```python
import functools
from typing import NamedTuple

import numpy as np
import jax
import jax.numpy as jnp
from jax import lax
from jax.experimental import pallas as pl
from jax.experimental.pallas import tpu as pltpu

F32 = jnp.float32
BF16 = jnp.bfloat16

EPS = 1e-6
MIN_FORGET = 1e-6
GRID_W = 64
ROPE_THETA = 10000.0
HEAD_DIM = 128
ATT_GROUP = 4
N_EXPERTS = 8
TOP_K = 2

V7X_LANES = 128
V7X_SUBLANES = 8
V7X_VMEM_LIMIT_BYTES = 56 * 1024 * 1024

HG_CHUNK = 64
HG_LEAF = 8
HG_BLOCK = 512
N_HG_MATS = 6


class Layout(NamedTuple):
    starts: tuple
    lens: tuple
    total: int


def _layout(x_prompt, x_sample):
    starts, lens, row = [], [], 0
    for arr in (x_prompt, x_sample):
        for _ in range(arr.shape[0]):
            starts.append(row)
            lens.append(arr.shape[1])
            row += arr.shape[1]
    return Layout(tuple(starts), tuple(lens), row)


def _seq_index(row, lay):
    s = jnp.int32(0)
    for st in lay.starts[1:]:
        s = s + (row >= st).astype(jnp.int32)
    return s


def _seq_start(row, lay):
    s = jnp.int32(0)
    for st in lay.starts[1:]:
        s = jnp.where(row >= st, jnp.int32(st), s)
    return s


def _tile(n, target, align):
    best = None
    t = align
    while t <= min(n, target):
        if n % t == 0:
            best = t
        t += align
    assert best is not None, (n, target, align)
    return best


def _row_tile(lay, target):
    g = 0
    for ln in lay.lens:
        g = np.gcd(g, ln)
    return _tile(int(g), target, 16)


def _params(*sem):
    return pltpu.CompilerParams(dimension_semantics=sem, vmem_limit_bytes=V7X_VMEM_LIMIT_BYTES)


def _silu(x):
    return x * (1.0 / (1.0 + jnp.exp(-x)))


def _lower_bounds_body(depth, l_ref, o_ref):
    for d in range(2):
        rows = [l_ref[d * depth + i:d * depth + i + 1, :] for i in range(depth)]
        m = functools.reduce(jnp.maximum, rows)
        e = [jnp.exp(r - m) for r in rows]
        tot = functools.reduce(lambda a, b: a + b, e)
        s = [ei / tot for ei in e]
        run = None
        for i in range(depth):
            run = s[i] if run is None else run + s[i]
            o_ref[d * depth + i:d * depth + i + 1, :] = run - s[0]


def _lower_bounds(lb_logits):
    two, depth, width = lb_logits.shape
    out = pl.pallas_call(
        functools.partial(_lower_bounds_body, depth),
        out_shape=jax.ShapeDtypeStruct((two * depth, width), F32),
        name="hgrn_lower_bounds",
    )(lb_logits.reshape(two * depth, width).astype(F32))
    return out.reshape(two, depth, width)


def _modulation_body(c_ref, w_ref, b_ref, o_ref):
    c = c_ref[...]
    sc = _silu(c).astype(BF16)
    acc = jnp.dot(sc, w_ref[0].astype(BF16), preferred_element_type=F32)
    o_ref[0] = acc + b_ref[0]


def _modulation(c_rows, w_mod, b_mod):
    depth, d, n = w_mod.shape
    rows = c_rows.shape[0]
    tn = _tile(n, 1024, V7X_LANES)
    return pl.pallas_call(
        _modulation_body,
        out_shape=jax.ShapeDtypeStruct((depth, rows, n), F32),
        grid=(depth, n // tn),
        in_specs=[
            pl.BlockSpec((rows, d), lambda l, j: (0, 0)),
            pl.BlockSpec((1, d, tn), lambda l, j: (l, 0, j)),
            pl.BlockSpec((1, 1, tn), lambda l, j: (l, 0, j)),
        ],
        out_specs=pl.BlockSpec((1, rows, tn), lambda l, j: (l, 0, j)),
        compiler_params=_params("parallel", "parallel"),
        name="adaln_modulation",
    )(c_rows, w_mod, b_mod.reshape(depth, 1, n))


def _mod_spec(lay, tm, layer, part, d, n_mod_rows):
    def index(m, *_):
        return (layer * n_mod_rows + _seq_index(m * tm, lay), 0, part)
    return pl.BlockSpec((1, 1, d), index)


def _norm_modulate(x, gain, scale, shift):
    ms = jnp.mean(x * x, axis=-1, keepdims=True)
    y = x * lax.rsqrt(ms + EPS) * gain
    return y * (1.0 + scale) + shift


def _in_proj_body(x_ref, sh_ref, sc_ref, gain_ref, w_ref, o_ref, h_scr):
    @pl.when(pl.program_id(1) == 0)
    def _():
        h = _norm_modulate(x_ref[...], gain_ref[...], sc_ref[0], sh_ref[0])
        h_scr[...] = h.astype(BF16)

    o_ref[...] = jnp.dot(h_scr[...], w_ref[...], preferred_element_type=F32).astype(o_ref.dtype)


def _in_proj(x, mod3, gain, w, lay, layer, n_mod_rows, out_dtype):
    t, d = x.shape
    n = w.shape[1]
    tm = _row_tile(lay, 512)
    tn = _tile(n, 512, V7X_LANES)
    return pl.pallas_call(
        _in_proj_body,
        out_shape=jax.ShapeDtypeStruct((t, n), out_dtype),
        grid=(t // tm, n // tn),
        in_specs=[
            pl.BlockSpec((tm, d), lambda m, j: (m, 0)),
            _mod_spec(lay, tm, layer, 0, d, n_mod_rows),
            _mod_spec(lay, tm, layer, 1, d, n_mod_rows),
            pl.BlockSpec((1, d), lambda m, j: (0, 0)),
            pl.BlockSpec((d, tn), lambda m, j: (0, j)),
        ],
        out_specs=pl.BlockSpec((tm, tn), lambda m, j: (m, j)),
        scratch_shapes=[pltpu.VMEM((tm, d), BF16)],
        compiler_params=_params("parallel", "arbitrary"),
        name="mixer_in_proj",
    )(x, mod3, mod3, gain.reshape(1, d), w)


def _hg_constants(c, leaf):
    t = np.arange(c)[:, None]
    u = np.arange(c)[None, :]
    prefix = (u <= t).astype(np.float32)
    rest = (u > t).astype(np.float32)
    mid = leaf * (t // leaf) + leaf // 2 - 1
    leaf_m = prefix - (u <= mid).astype(np.float32)
    mats = [prefix, rest, leaf_m]
    level = np.full((c, c), -1, np.int32)
    s = u
    level[(s <= t) & (t // leaf == s // leaf)] = 0
    w = leaf
    lvl = 1
    while w < c:
        bd = (t // (2 * w)) * 2 * w + w - 1
        right = t > bd
        m = np.where(right, (u > bd) & (u <= t), (u > t) & (u <= bd)).astype(np.float32)
        mats.append(m)
        level[(s <= t) & (t // (2 * w) == s // (2 * w)) & (t // w != s // w)] = lvl
        w *= 2
        lvl += 1
    fwd = np.concatenate(mats, axis=0)
    bwd = np.concatenate([m[::-1, ::-1] for m in mats], axis=0)
    return fwd, bwd, level, level[::-1, ::-1].copy(), lvl


def _gates(z, lb):
    t = jnp.exp(-jnp.abs(z))
    r = 1.0 / (1.0 + t)
    pos = z >= 0
    sig = jnp.where(pos, r, t * r)
    sig_neg = jnp.where(pos, t * r, r)
    f = lb + (1.0 - lb) * sig
    logf = jnp.log(jnp.maximum(f, MIN_FORGET))
    k = (1.0 - lb) * sig_neg
    return logf, k


def _hg_chunk(q, z, v, lb, mats, level, n_levels, state_t, last_row):
    c = q.shape[0]
    logf, k = _gates(z, lb)
    hi = logf.astype(BF16)
    lo = (logf - hi.astype(F32)).astype(BF16)
    e2 = jnp.dot(mats, jnp.concatenate([hi, lo], axis=1), preferred_element_type=F32)
    e = e2[:, :HEAD_DIM] + e2[:, HEAD_DIM:]
    e_a = e[0:c]
    e_rest = e[c:2 * c]
    e_leaf = e[2 * c:3 * c]
    nt = (((1,), (1,)), ((), ()))
    q_leaf = (q * jnp.exp(e_leaf)).astype(BF16)
    k_leaf = (k * jnp.exp(-e_leaf)).astype(BF16)
    p = lax.dot_general(q_leaf, k_leaf, nt, preferred_element_type=F32)
    scores = jnp.where(level == 0, p, 0.0)
    for lvl in range(1, n_levels):
        x = jnp.exp(e[(2 + lvl) * c:(3 + lvl) * c])
        p = lax.dot_general((q * x).astype(BF16), (k * x).astype(BF16), nt, preferred_element_type=F32)
        scores = jnp.where(level == lvl, p, scores)
    v16 = v.astype(BF16)
    o = jnp.dot(scores.astype(BF16), v16, preferred_element_type=F32)
    q_in = (q * jnp.exp(e_a)).astype(BF16)
    o = o + lax.dot_general(q_in, state_t.astype(BF16), nt, preferred_element_type=F32)
    k_out = (k * jnp.exp(e_rest)).astype(BF16)
    tn = (((0,), (0,)), ((), ()))
    upd = lax.dot_general(v16, k_out, tn, preferred_element_type=F32)
    a_last = e_a[last_row:last_row + 1, :]
    return o, state_t * jnp.exp(a_last) + upd


def _hgrn_body(lay, nb, n_levels, q_ref, zf_ref, zb_ref, v_ref, g_ref, lbf_ref, lbb_ref, gain_ref,
               mf_ref, mb_ref, lvf_ref, lvb_ref, o_ref, state_scr, fwd_scr):
    phase = pl.program_id(1)
    step = pl.program_id(2)
    c = HG_CHUNK
    n_chunks = q_ref.shape[0] // c

    def run(z_ref, lb_ref, m_ref, lv_ref, blk, reverse):
        row0 = blk * (n_chunks * c)
        bounds = [st + ln for st, ln in zip(lay.starts, lay.lens)] if reverse else list(lay.starts)
        edge = row0 + n_chunks * c if reverse else row0
        reset = functools.reduce(jnp.logical_or, [edge == b for b in bounds])

        @pl.when(reset)
        def _():
            state_scr[...] = jnp.zeros_like(state_scr)

        lb = lb_ref[0]
        mats = m_ref[...]
        level = lv_ref[...]

        def chunk(j, state_t):
            jj = (n_chunks - 1 - j) if reverse else j
            rows = pl.ds(pl.multiple_of(jj * c, c), c)
            q = q_ref[rows, :].astype(F32)
            z = z_ref[rows, :].astype(F32)
            v = v_ref[rows, :].astype(F32)
            o, new_state = _hg_chunk(q, z, v, lb, mats, level, n_levels, state_t,
                                     0 if reverse else c - 1)
            all_rows = pl.ds(pl.multiple_of(row0 + jj * c, c), c)
            if reverse:
                tot = fwd_scr[all_rows, :] + o
                ms = jnp.mean(tot * tot, axis=-1, keepdims=True)
                y = tot * lax.rsqrt(ms + EPS) * gain_ref[...]
                o_ref[rows, :] = (y * _silu(g_ref[rows, :].astype(F32))).astype(o_ref.dtype)
            else:
                fwd_scr[all_rows, :] = o
            return new_state

        state_scr[...] = lax.fori_loop(0, n_chunks, chunk, state_scr[...])

    @pl.when(phase == 0)
    def _():
        run(zf_ref, lbf_ref, mf_ref, lvf_ref, step, False)

    @pl.when(phase == 1)
    def _():
        run(zb_ref, lbb_ref, mb_ref, lvb_ref, nb - 1 - step, True)


def _hgrn(proj, lbs_f, lbs_b, out_gain, lay, n_heads, out_dtype):
    t = proj.shape[0]
    lb_rows = _tile(int(np.gcd.reduce(lay.lens)), HG_BLOCK, HG_CHUNK)
    nb = t // lb_rows
    fwd, bwd, lv_f, lv_b, n_levels = _hg_constants(HG_CHUNK, HG_LEAF)
    assert fwd.shape[0] == N_HG_MATS * HG_CHUNK

    def rows(p, s):
        return jnp.where(p == 0, s, nb - 1 - s)

    def col(group):
        return lambda h, p, s: (rows(p, s), group * n_heads + h)

    def fwd_only(group):
        return lambda h, p, s: (jnp.where(p == 0, s, nb - 1), group * n_heads + h)

    def bwd_only(group):
        return lambda h, p, s: (jnp.where(p == 0, nb - 1, nb - 1 - s), group * n_heads + h)

    blk = (lb_rows, HEAD_DIM)
    const = lambda h, p, s: (0, 0)
    return pl.pallas_call(
        functools.partial(_hgrn_body, lay, nb, n_levels),
        out_shape=jax.ShapeDtypeStruct((t, n_heads * HEAD_DIM), out_dtype),
        grid=(n_heads, 2, nb),
        in_specs=[
            pl.BlockSpec(blk, col(0)),
            pl.BlockSpec(blk, fwd_only(1)),
            pl.BlockSpec(blk, bwd_only(2)),
            pl.BlockSpec(blk, col(3)),
            pl.BlockSpec(blk, bwd_only(4)),
            pl.BlockSpec((1, 1, HEAD_DIM), lambda h, p, s: (h, 0, 0)),
            pl.BlockSpec((1, 1, HEAD_DIM), lambda h, p, s: (h, 0, 0)),
            pl.BlockSpec((1, HEAD_DIM), const),
            pl.BlockSpec(fwd.shape, const),
            pl.BlockSpec(bwd.shape, const),
            pl.BlockSpec(lv_f.shape, const),
            pl.BlockSpec(lv_b.shape, const),
        ],
        out_specs=pl.BlockSpec(blk, lambda h, p, s: (jnp.where(p == 0, nb - 1, nb - 1 - s), h)),
        scratch_shapes=[pltpu.VMEM((HEAD_DIM, HEAD_DIM), F32), pltpu.VMEM((t, HEAD_DIM), F32)],
        compiler_params=_params("arbitrary", "arbitrary", "arbitrary"),
        name="hgrn2_scan",
    )(proj, proj, proj, proj, proj,
      lbs_f.reshape(n_heads, 1, HEAD_DIM), lbs_b.reshape(n_heads, 1, HEAD_DIM),
      out_gain.reshape(1, HEAD_DIM),
      jnp.asarray(fwd, BF16), jnp.asarray(bwd, BF16), jnp.asarray(lv_f), jnp.asarray(lv_b))


def _rope_tables(n_pos):
    n_freq = HEAD_DIM // 4
    pos = jnp.arange(n_pos)
    coords = jnp.stack([pos // GRID_W, pos % GRID_W], axis=-1).astype(F32)
    inv = ROPE_THETA ** (-jnp.arange(n_freq, dtype=F32) / n_freq)
    ang = coords[:, :, None] * inv
    cos = jnp.broadcast_to(jnp.cos(ang)[:, :, None, :], (n_pos, 2, 2, n_freq))
    sin = jnp.sin(ang)
    sin = jnp.stack([-sin, sin], axis=2)
    return cos.reshape(n_pos, HEAD_DIM), sin.reshape(n_pos, HEAD_DIM)


def _qk_prep_body(n_q, n_kv, q_ref, k_ref, v_ref, cos_ref, sin_ref, qg_ref, kg_ref,
                  qo_ref, ko_ref, vo_ref):
    cos = cos_ref[...]
    sin = sin_ref[...]
    lane = lax.broadcasted_iota(jnp.int32, cos.shape, 1)
    first_half = (lane % (HEAD_DIM // 2)) < (HEAD_DIM // 4)

    def norm_rope(x, gain, scale):
        ms = jnp.mean(x * x, axis=-1, keepdims=True)
        y = x * lax.rsqrt(ms + EPS) * gain
        quarter = HEAD_DIM // 4
        partner = jnp.where(first_half, pltpu.roll(y, HEAD_DIM - quarter, 1), pltpu.roll(y, quarter, 1))
        out = y * cos + partner * sin
        return out * scale if scale is not None else out

    for h in range(n_q):
        cols = slice(h * HEAD_DIM, (h + 1) * HEAD_DIM)
        qo_ref[:, cols] = norm_rope(q_ref[:, cols].astype(F32), qg_ref[...], HEAD_DIM ** -0.5).astype(qo_ref.dtype)
    for h in range(n_kv):
        cols = slice(h * HEAD_DIM, (h + 1) * HEAD_DIM)
        ko_ref[:, cols] = norm_rope(k_ref[:, cols].astype(F32), kg_ref[...], None).astype(ko_ref.dtype)
    vo_ref[...] = v_ref[...].astype(vo_ref.dtype)


def _qk_prep(proj, q_gain, k_gain, cos, sin, lay, q_col, n_q, n_kv):
    t = proj.shape[0]
    tm = _row_tile(lay, 256)
    qw, kw = n_q * HEAD_DIM, n_kv * HEAD_DIM
    assert q_col % qw == 0 and (q_col + qw) % kw == 0

    def pos_block(m):
        row = m * tm
        return ((row - _seq_start(row, lay)) // tm, 0)

    return pl.pallas_call(
        functools.partial(_qk_prep_body, n_q, n_kv),
        out_shape=(jax.ShapeDtypeStruct((t, qw), BF16), jax.ShapeDtypeStruct((t, kw), BF16),
                   jax.ShapeDtypeStruct((t, kw), BF16)),
        grid=(t // tm,),
        in_specs=[
            pl.BlockSpec((tm, qw), lambda m: (m, q_col // qw)),
            pl.BlockSpec((tm, kw), lambda m: (m, (q_col + qw) // kw)),
            pl.BlockSpec((tm, kw), lambda m: (m, (q_col + qw) // kw + 1)),
            pl.BlockSpec((tm, HEAD_DIM), pos_block),
            pl.BlockSpec((tm, HEAD_DIM), pos_block),
            pl.BlockSpec((1, HEAD_DIM), lambda m: (0, 0)),
            pl.BlockSpec((1, HEAD_DIM), lambda m: (0, 0)),
        ],
        out_specs=(pl.BlockSpec((tm, qw), lambda m: (m, 0)), pl.BlockSpec((tm, kw), lambda m: (m, 0)),
                   pl.BlockSpec((tm, kw), lambda m: (m, 0))),
        compiler_params=_params("parallel"),
        name="attn_qk_prep",
    )(proj, proj, proj, cos, sin, q_gain.reshape(1, HEAD_DIM), k_gain.reshape(1, HEAD_DIM))


def _flash_body(qb_ref, kb_ref, flag_ref, q_ref, k_ref, v_ref, o_ref, m_scr, l_scr, acc_scr):
    i = pl.program_id(1)
    flags = flag_ref[i]
    tq = q_ref.shape[0]

    @pl.when((flags & 1) != 0)
    def _():
        m_scr[...] = jnp.full_like(m_scr, -jnp.inf)
        l_scr[...] = jnp.zeros_like(l_scr)
        acc_scr[...] = jnp.zeros_like(acc_scr)

    q = q_ref[...]
    qs = jnp.concatenate([q[:, g * HEAD_DIM:(g + 1) * HEAD_DIM] for g in range(ATT_GROUP)], axis=0)
    s = lax.dot_general(qs, k_ref[...], (((1,), (1,)), ((), ())), preferred_element_type=F32)
    m_prev = m_scr[...]
    m_new = jnp.maximum(m_prev, jnp.max(s, axis=-1, keepdims=True))
    alpha = jnp.exp(m_prev - m_new)
    p = jnp.exp(s - m_new)
    l_scr[...] = alpha * l_scr[...] + jnp.sum(p, axis=-1, keepdims=True)
    acc_scr[...] = alpha * acc_scr[...] + jnp.dot(p.astype(BF16), v_ref[...], preferred_element_type=F32)
    m_scr[...] = m_new

    @pl.when((flags & 2) != 0)
    def _():
        out = acc_scr[...] / l_scr[...]
        for g in range(ATT_GROUP):
            o_ref[:, g * HEAD_DIM:(g + 1) * HEAD_DIM] = out[g * tq:(g + 1) * tq].astype(o_ref.dtype)


def _flash_attention(q_rot, k_rot, v16, lay, n_kv, out_dtype):
    t = q_rot.shape[0]
    g = int(np.gcd.reduce(lay.lens))
    tq = _tile(g, 256, 16)
    tk = _tile(g, 1024, V7X_LANES)
    qb, kb, flags = [], [], []
    for st, ln in zip(lay.starts, lay.lens):
        for qi in range(ln // tq):
            for ki in range(ln // tk):
                qb.append(st // tq + qi)
                kb.append(st // tk + ki)
                flags.append((1 if ki == 0 else 0) | (2 if ki == ln // tk - 1 else 0))
    n_items = len(qb)
    gw = ATT_GROUP * HEAD_DIM
    grid_spec = pltpu.PrefetchScalarGridSpec(
        num_scalar_prefetch=3,
        grid=(n_kv, n_items),
        in_specs=[
            pl.BlockSpec((tq, gw), lambda h, i, qb, kb, fl: (qb[i], h)),
            pl.BlockSpec((tk, HEAD_DIM), lambda h, i, qb, kb, fl: (kb[i], h)),
            pl.BlockSpec((tk, HEAD_DIM), lambda h, i, qb, kb, fl: (kb[i], h)),
        ],
        out_specs=pl.BlockSpec((tq, gw), lambda h, i, qb, kb, fl: (qb[i], h)),
        scratch_shapes=[pltpu.VMEM((ATT_GROUP * tq, 1), F32), pltpu.VMEM((ATT_GROUP * tq, 1), F32),
                        pltpu.VMEM((ATT_GROUP * tq, HEAD_DIM), F32)],
    )
    return pl.pallas_call(
        _flash_body,
        out_shape=jax.ShapeDtypeStruct((t, n_kv * gw), out_dtype),
        grid_spec=grid_spec,
        compiler_params=_params("parallel", "arbitrary"),
        name="attn_flash",
    )(jnp.asarray(qb, jnp.int32), jnp.asarray(kb, jnp.int32), jnp.asarray(flags, jnp.int32),
      q_rot, k_rot, v16)


def _out_proj_body(a_ref, b_ref, wa_ref, wb_ref, x_ref, g_ref, o_ref):
    acc = jnp.dot(a_ref[...], wa_ref[...], preferred_element_type=F32)
    acc = acc + jnp.dot(b_ref[...], wb_ref[...], preferred_element_type=F32)
    o_ref[...] = x_ref[...] + g_ref[0] * acc


def _out_proj(o_hg, o_att, w_out, x, mod3, lay, layer, n_mod_rows):
    t, d = x.shape
    ka, kb = o_hg.shape[1], o_att.shape[1]
    tm = _row_tile(lay, 512)
    return pl.pallas_call(
        _out_proj_body,
        out_shape=jax.ShapeDtypeStruct((t, d), F32),
        grid=(t // tm,),
        in_specs=[
            pl.BlockSpec((tm, ka), lambda m: (m, 0)),
            pl.BlockSpec((tm, kb), lambda m: (m, 0)),
            pl.BlockSpec((ka, d), lambda m: (0, 0)),
            pl.BlockSpec((kb, d), lambda m: (1, 0)),
            pl.BlockSpec((tm, d), lambda m: (m, 0)),
            _mod_spec(lay, tm, layer, 2, d, n_mod_rows),
        ],
        out_specs=pl.BlockSpec((tm, d), lambda m: (m, 0)),
        compiler_params=_params("parallel"),
        name="mixer_out_proj",
    )(o_hg, o_att, w_out, w_out, x, mod3)


def _ffn_in_body(x_ref, sh_ref, sc_ref, gain_ref, o_ref):
    o_ref[...] = _norm_modulate(x_ref[...], gain_ref[...], sc_ref[0], sh_ref[0]).astype(o_ref.dtype)


def _router_body(x_ref, sh_ref, sc_ref, gain_ref, wr_ref, o_ref, idx_ref, gate_ref):
    h = _norm_modulate(x_ref[...], gain_ref[...], sc_ref[0], sh_ref[0])
    o_ref[...] = h.astype(o_ref.dtype)
    logits = jnp.dot(h, wr_ref[...], preferred_element_type=F32, precision=lax.Precision.HIGHEST)
    n_e = logits.shape[1]
    lane = lax.broadcasted_iota(jnp.int32, logits.shape, 1)
    m1 = jnp.max(logits, axis=-1, keepdims=True)
    i1 = jnp.min(jnp.where(logits == m1, lane, n_e), axis=-1, keepdims=True)
    rest = jnp.where(lane == i1, -jnp.inf, logits)
    m2 = jnp.max(rest, axis=-1, keepdims=True)
    i2 = jnp.min(jnp.where(rest == m2, lane, n_e), axis=-1, keepdims=True)
    e = jnp.exp(m2 - m1)
    tot = 1.0 + e
    two = lax.broadcasted_iota(jnp.int32, idx_ref.shape, 1)
    idx_ref[...] = jnp.where(two == 0, i1, i2)
    gate_ref[...] = jnp.where(two == 0, 1.0 / tot, e / tot)


def _ffn_in(x, mod3, gain, lay, layer, n_mod_rows, w_router=None):
    t, d = x.shape
    tm = _row_tile(lay, 256)
    in_specs = [
        pl.BlockSpec((tm, d), lambda m: (m, 0)),
        _mod_spec(lay, tm, layer, 3, d, n_mod_rows),
        _mod_spec(lay, tm, layer, 4, d, n_mod_rows),
        pl.BlockSpec((1, d), lambda m: (0, 0)),
    ]
    row_spec = pl.BlockSpec((tm, d), lambda m: (m, 0))
    if w_router is None:
        return pl.pallas_call(
            _ffn_in_body,
            out_shape=jax.ShapeDtypeStruct((t, d), BF16),
            grid=(t // tm,), in_specs=in_specs, out_specs=row_spec,
            compiler_params=_params("parallel"), name="ffn_in",
        )(x, mod3, mod3, gain.reshape(1, d))
    n_e = w_router.shape[1]
    small = pl.BlockSpec((tm, TOP_K), lambda m: (m, 0))
    return pl.pallas_call(
        _router_body,
        out_shape=(jax.ShapeDtypeStruct((t, d), F32), jax.ShapeDtypeStruct((t, TOP_K), jnp.int32),
                   jax.ShapeDtypeStruct((t, TOP_K), F32)),
        grid=(t // tm,),
        in_specs=in_specs + [pl.BlockSpec((d, n_e), lambda m: (0, 0))],
        out_specs=(row_spec, small, small),
        compiler_params=_params("parallel"), name="moe_router",
    )(x, mod3, mod3, gain.reshape(1, d), w_router)


def _swiglu_body(residual, be_ref, nu_ref, x_ref, wg_ref, wu_ref, wd_ref, *rest):
    if residual:
        res_ref, g_ref, o_ref, x16_scr, acc_scr = rest
    else:
        o_ref, x16_scr, acc_scr = rest
    m = pl.program_id(0)
    f = pl.program_id(1)
    used = m < nu_ref[0]
    last = f == pl.num_programs(1) - 1

    @pl.when(jnp.logical_and(used, f == 0))
    def _():
        x16_scr[...] = x_ref[...].astype(BF16)
        acc_scr[...] = jnp.zeros_like(acc_scr)

    @pl.when(used)
    def _():
        x16 = x16_scr[...]
        gate = jnp.dot(x16, wg_ref[0], preferred_element_type=F32)
        up = jnp.dot(x16, wu_ref[0], preferred_element_type=F32)
        act = (_silu(gate) * up).astype(BF16)
        acc_scr[...] += jnp.dot(act, wd_ref[0], preferred_element_type=F32)

    @pl.when(jnp.logical_and(used, last))
    def _():
        if residual:
            o_ref[...] = res_ref[...] + g_ref[0] * acc_scr[...]
        else:
            o_ref[...] = acc_scr[...].astype(o_ref.dtype)

    @pl.when(jnp.logical_and(jnp.logical_not(used), last))
    def _():
        o_ref[...] = jnp.zeros_like(o_ref)


def _swiglu(x_rows, w_gate, w_up, w_down, block_expert, n_used, tm, residual=None):
    r, d = x_rows.shape
    fdim = w_gate.shape[2]
    tf = _tile(fdim, 512, V7X_LANES)
    nf = fdim // tf
    nblk = r // tm

    def xrow(m, f, be, nu):
        return (jnp.minimum(m, nu[0] - 1), 0)

    def fcol(m, f, nu):
        return jnp.where(m < nu[0], f, nf - 1)

    in_specs = [
        pl.BlockSpec((tm, d), xrow),
        pl.BlockSpec((1, d, tf), lambda m, f, be, nu: (be[m], 0, fcol(m, f, nu))),
        pl.BlockSpec((1, d, tf), lambda m, f, be, nu: (be[m], 0, fcol(m, f, nu))),
        pl.BlockSpec((1, tf, d), lambda m, f, be, nu: (be[m], fcol(m, f, nu), 0)),
    ]
    args = [x_rows, w_gate, w_up, w_down]
    out_dtype = F32
    if residual is not None:
        x_res, mod3, lay, layer, n_mod_rows = residual
        in_specs += [pl.BlockSpec((tm, d), lambda m, f, be, nu: (m, 0)),
                     _mod_spec(lay, tm, layer, 5, d, n_mod_rows)]
        args += [x_res, mod3]
    grid_spec = pltpu.PrefetchScalarGridSpec(
        num_scalar_prefetch=2,
        grid=(nblk, nf),
        in_specs=in_specs,
        out_specs=pl.BlockSpec((tm, d), lambda m, f, be, nu: (m, 0)),
        scratch_shapes=[pltpu.VMEM((tm, d), BF16), pltpu.VMEM((tm, d), F32)],
    )
    return pl.pallas_call(
        functools.partial(_swiglu_body, residual is not None),
        out_shape=jax.ShapeDtypeStruct((r, d), out_dtype),
        grid_spec=grid_spec,
        compiler_params=_params("parallel", "arbitrary"),
        name="swiglu_residual" if residual is not None else "swiglu_experts",
    )(block_expert, n_used, *args)


DISPATCH_ROWS = 512


def _dispatch_body(dest_ref, h_hbm, xb_in_hbm, xb_hbm, sem):
    del xb_in_hbm
    i = pl.program_id(0)
    n_steps = pl.num_programs(0)
    base = i * DISPATCH_ROWS

    def row_copy(src_row, dst_row):
        return pltpu.make_async_copy(h_hbm.at[pl.ds(src_row, 1)], xb_hbm.at[pl.ds(dst_row, 1)], sem)

    def issue(s, carry):
        slot = base + s
        row_copy(slot // TOP_K, dest_ref[slot]).start()
        return carry

    def drain(s, carry):
        row_copy(0, 0).wait()
        return carry

    lax.fori_loop(0, DISPATCH_ROWS, issue, 0)

    @pl.when(i > 0)
    def _():
        lax.fori_loop(0, DISPATCH_ROWS, drain, 0)

    @pl.when(i == n_steps - 1)
    def _():
        lax.fori_loop(0, DISPATCH_ROWS, drain, 0)


def _dispatch(h, dest, cap):
    t, d = h.shape
    n_slots = dest.shape[0]
    assert n_slots % DISPATCH_ROWS == 0
    grid_spec = pltpu.PrefetchScalarGridSpec(
        num_scalar_prefetch=1,
        grid=(n_slots // DISPATCH_ROWS,),
        in_specs=[pl.BlockSpec(memory_space=pl.ANY), pl.BlockSpec(memory_space=pl.ANY)],
        out_specs=pl.BlockSpec(memory_space=pl.ANY),
        scratch_shapes=[pltpu.SemaphoreType.DMA],
    )
    return pl.pallas_call(
        _dispatch_body,
        out_shape=jax.ShapeDtypeStruct((cap, d), h.dtype),
        grid_spec=grid_spec,
        input_output_aliases={2: 0},
        compiler_params=_params("arbitrary"),
        name="moe_dispatch",
    )(dest, h, jnp.zeros((cap, d), h.dtype))


def _combine_body(dest_ref, yb_hbm, x_ref, gate_ref, g_ref, o_ref, buf, sem):
    tm = x_ref.shape[0]
    base = pl.program_id(0) * tm

    def row_copy(src_row, k, dst_row):
        return pltpu.make_async_copy(yb_hbm.at[pl.ds(src_row, 1)], buf.at[k, pl.ds(dst_row, 1)], sem)

    def issue(r, carry):
        for k in range(TOP_K):
            row_copy(dest_ref[(base + r) * TOP_K + k], k, r).start()
        return carry

    def drain(r, carry):
        for k in range(TOP_K):
            row_copy(0, k, r).wait()
        return carry

    lax.fori_loop(0, tm, issue, 0)
    lax.fori_loop(0, tm, drain, 0)
    gates = gate_ref[...]
    y = buf[0] * gates[:, 0:1]
    for k in range(1, TOP_K):
        y = y + buf[k] * gates[:, k:k + 1]
    o_ref[...] = x_ref[...] + g_ref[0] * y


def _combine(yb, dest, gates, x, mod3, lay, layer, n_mod_rows):
    t, d = x.shape
    tm = _row_tile(lay, 256)
    grid_spec = pltpu.PrefetchScalarGridSpec(
        num_scalar_prefetch=1,
        grid=(t // tm,),
        in_specs=[
            pl.BlockSpec(memory_space=pl.ANY),
            pl.BlockSpec((tm, d), lambda m, dest: (m, 0)),
            pl.BlockSpec((tm, TOP_K), lambda m, dest: (m, 0)),
            _mod_spec(lay, tm, layer, 5, d, n_mod_rows),
        ],
        out_specs=pl.BlockSpec((tm, d), lambda m, dest: (m, 0)),
        scratch_shapes=[pltpu.VMEM((TOP_K, tm, d), yb.dtype), pltpu.SemaphoreType.DMA],
    )
    return pl.pallas_call(
        _combine_body,
        out_shape=jax.ShapeDtypeStruct((t, d), F32),
        grid_spec=grid_spec,
        compiler_params=_params("arbitrary"),
        name="moe_combine",
    )(dest, yb, x, gates, mod3)


def _moe(x, h32, idx, gates, w_gate, w_up, w_down, first_expert, n_e, mod3, lay, layer, n_mod_rows):
    t, d = x.shape
    tm = _row_tile(lay, 512)
    n_slots = t * TOP_K
    cap = n_slots + n_e * tm
    e = idx.reshape(-1)
    onehot = (e[:, None] == jnp.arange(n_e, dtype=jnp.int32)[None, :]).astype(jnp.int32)
    csum = jnp.cumsum(onehot, axis=0)
    rank = jnp.take_along_axis(csum, e[:, None], axis=1)[:, 0] - 1
    counts = csum[-1]
    padded = (counts + tm - 1) // tm * tm
    pad_end = jnp.cumsum(padded)
    pad_start = pad_end - padded
    dest = (pad_start[e] + rank).astype(jnp.int32)
    block_first = jnp.arange(cap // tm, dtype=jnp.int32) * tm
    block_e = jnp.minimum(jnp.searchsorted(pad_end, block_first, side='right'), n_e - 1).astype(jnp.int32)
    n_used = (pad_end[-1:] // tm).astype(jnp.int32)
    block_e = jnp.where(block_first < pad_end[-1], block_e, block_e[jnp.maximum(n_used[0] - 1, 0)])
    xb = _dispatch(h32, dest, cap)
    yb = _swiglu(xb, w_gate, w_up, w_down, block_e + first_expert, n_used, tm)
    return _combine(yb, dest, gates, x, mod3, lay, layer, n_mod_rows)


def _final_norm_body(x_ref, gain_ref, o_ref):
    x = x_ref[...]
    ms = jnp.mean(x * x, axis=-1, keepdims=True)
    o_ref[...] = x * lax.rsqrt(ms + EPS) * gain_ref[...]


def _final_norm(x, gain, row0, n_rows, tm):
    d = x.shape[1]
    first = row0 // tm
    return pl.pallas_call(
        _final_norm_body,
        out_shape=jax.ShapeDtypeStruct((n_rows, d), F32),
        grid=(n_rows // tm,),
        in_specs=[pl.BlockSpec((tm, d), lambda m: (first + m, 0)), pl.BlockSpec((1, d), lambda m: (0, 0))],
        out_specs=pl.BlockSpec((tm, d), lambda m: (m, 0)),
        compiler_params=_params("parallel"),
        name="final_norm",
    )(x, gain.reshape(1, d))


MOD_ROWS = 8
PROJ_DTYPE = F32


def kernel(x_prompt, x_sample, c_prompt, c_sample, w_mod, b_mod, norm_mix, norm_ffn, w_in, hg_lb_logits,
           hg_out_norm, qk_norm, w_out, ffn_gate, ffn_up, ffn_down, router, exp_gate, exp_up, exp_down,
           final_norm):
    depth, d = norm_mix.shape
    lay = _layout(x_prompt, x_sample)
    n_seq = len(lay.starts)
    assert n_seq <= MOD_ROWS
    hg_heads = hg_lb_logits.shape[2] // HEAD_DIM
    kv_heads = (w_in.shape[2] - 5 * hg_heads * HEAD_DIM) // HEAD_DIM
    kv_heads = kv_heads // (ATT_GROUP + 2)
    att_heads = kv_heads * ATT_GROUP
    hg_width = hg_heads * HEAD_DIM

    x = jnp.concatenate([x_prompt.reshape(-1, d), x_sample.reshape(-1, d)], axis=0)
    c_rows = jnp.concatenate([c_prompt, c_sample, jnp.zeros((MOD_ROWS - n_seq, d), F32)], axis=0)
    mod3 = _modulation(c_rows, w_mod, b_mod).reshape(depth * MOD_ROWS, 1, 6 * d)
    lbs = _lower_bounds(hg_lb_logits)
    cos, sin = _rope_tables(max(lay.lens))

    w_in16 = w_in.astype(BF16)
    w_out16 = w_out.astype(BF16)
    n_e = exp_gate.shape[1]
    dense16 = [w.astype(BF16) for w in (ffn_gate, ffn_up, ffn_down)]
    expert16 = [w.astype(BF16).reshape((-1,) + w.shape[2:]) for w in (exp_gate, exp_up, exp_down)]

    tm_dense = _row_tile(lay, 512)
    dense_used = jnp.full((1,), lay.total // tm_dense, jnp.int32)

    for layer in range(depth):
        proj = _in_proj(x, mod3, norm_mix[layer], w_in16[layer], lay, layer, MOD_ROWS, PROJ_DTYPE)
        o_hg = _hgrn(proj, lbs[0, layer], lbs[1, layer], hg_out_norm[layer], lay, hg_heads, BF16)
        q_rot, k_rot, v16 = _qk_prep(proj, qk_norm[layer, 0], qk_norm[layer, 1], cos, sin, lay,
                                     5 * hg_width, att_heads, kv_heads)
        o_att = _flash_attention(q_rot, k_rot, v16, lay, kv_heads, BF16)
        x = _out_proj(o_hg, o_att, w_out16[layer], x, mod3, lay, layer, MOD_ROWS)
        j = layer // 2
        if layer % 2 == 0:
            h16 = _ffn_in(x, mod3, norm_ffn[layer], lay, layer, MOD_ROWS)
            dense_blocks = jnp.full((lay.total // tm_dense,), j, jnp.int32)
            x = _swiglu(h16, *dense16, dense_blocks, dense_used, tm_dense,
                        residual=(x, mod3, lay, layer, MOD_ROWS))
        else:
            h32, idx, gates = _ffn_in(x, mod3, norm_ffn[layer], lay, layer, MOD_ROWS, w_router=router[j])
            x = _moe(x, h32, idx, gates, *expert16, j * n_e, n_e, mod3, lay, layer, MOD_ROWS)

    tm = _row_tile(lay, 512)
    n_prompt = x_prompt.shape[0] * x_prompt.shape[1]
    y_prompt = _final_norm(x, final_norm, 0, n_prompt, tm).reshape(x_prompt.shape)
    y_sample = _final_norm(x, final_norm, n_prompt, lay.total - n_prompt, tm).reshape(x_sample.shape)
    return (y_prompt, y_sample)
```

```python
import functools
from typing import NamedTuple

import numpy as np
import jax
import jax.numpy as jnp
from jax import lax
from jax.experimental import pallas as pl
from jax.experimental.pallas import tpu as pltpu

F32 = jnp.float32
BF16 = jnp.bfloat16

EPS = 1e-6
MIN_FORGET = 1e-6
GRID_W = 64
ROPE_THETA = 10000.0
HEAD_DIM = 128
ATT_GROUP = 4
TOP_K = 2
LOG2_E = 1.4426950408889634

V7X_LANES = 128
V7X_MXU_COLS = 256
V7X_VMEM_LIMIT_BYTES = 56 * 1024 * 1024

FLASH_KV_ROWS = 4096
HG_CHUNK = 64
HG_LEAF = 8
HG_BLOCK = 512
HG_HEADS_PER_STEP = 8
HG_UNROLL = 4
MOD_ROWS = 8
PROJ_DTYPE = BF16


class Layout(NamedTuple):
    starts: tuple
    lens: tuple
    total: int


def _layout(x_prompt, x_sample):
    starts, lens, row = [], [], 0
    for arr in (x_prompt, x_sample):
        for _ in range(arr.shape[0]):
            starts.append(row)
            lens.append(arr.shape[1])
            row += arr.shape[1]
    return Layout(tuple(starts), tuple(lens), row)


def _seq_index(row, lay):
    s = jnp.int32(0)
    for st in lay.starts[1:]:
        s = s + (row >= st).astype(jnp.int32)
    return s


def _seq_start(row, lay):
    s = jnp.int32(0)
    for st in lay.starts[1:]:
        s = jnp.where(row >= st, jnp.int32(st), s)
    return s


def _tile(n, target, align):
    best = None
    t = align
    while t <= min(n, target):
        if n % t == 0:
            best = t
        t += align
    assert best is not None, (n, target, align)
    return best


def _row_tile(lay, target):
    return _tile(int(np.gcd.reduce(lay.lens)), target, 16)


def _params(*sem):
    return pltpu.CompilerParams(dimension_semantics=sem, vmem_limit_bytes=V7X_VMEM_LIMIT_BYTES)


def _silu(x):
    return x * (1.0 / (1.0 + jnp.exp(-x)))


def _lower_bounds_body(depth, l_ref, o_ref):
    for d in range(2):
        rows = [l_ref[d * depth + i:d * depth + i + 1, :] for i in range(depth)]
        m = functools.reduce(jnp.maximum, rows)
        e = [jnp.exp(r - m) for r in rows]
        tot = functools.reduce(lambda a, b: a + b, e)
        s = [ei / tot for ei in e]
        run = None
        for i in range(depth):
            run = s[i] if run is None else run + s[i]
            o_ref[d * depth + i:d * depth + i + 1, :] = run - s[0]


def _lower_bounds(lb_logits):
    two, depth, width = lb_logits.shape
    out = pl.pallas_call(
        functools.partial(_lower_bounds_body, depth),
        out_shape=jax.ShapeDtypeStruct((two * depth, width), F32),
        name="hgrn_lower_bounds",
    )(lb_logits.reshape(two * depth, width).astype(F32))
    return out.reshape(two, depth, width)


def _modulation_body(c_ref, w_ref, b_ref, o_ref):
    sc = _silu(c_ref[...]).astype(BF16)
    acc = jnp.dot(sc, w_ref[0].astype(BF16), preferred_element_type=F32)
    o_ref[0] = acc + b_ref[0]


def _modulation(c_rows, w_mod, b_mod):
    depth, d, n = w_mod.shape
    rows = c_rows.shape[0]
    tn = _tile(n, 1024, V7X_LANES)
    return pl.pallas_call(
        _modulation_body,
        out_shape=jax.ShapeDtypeStruct((depth, rows, n), F32),
        grid=(depth, n // tn),
        in_specs=[
            pl.BlockSpec((rows, d), lambda l, j: (0, 0)),
            pl.BlockSpec((1, d, tn), lambda l, j: (l, 0, j)),
            pl.BlockSpec((1, 1, tn), lambda l, j: (l, 0, j)),
        ],
        out_specs=pl.BlockSpec((1, rows, tn), lambda l, j: (l, 0, j)),
        compiler_params=_params("parallel", "parallel"),
        name="adaln_modulation",
    )(c_rows, w_mod, b_mod.reshape(depth, 1, n))


def _mod_spec(lay, tm, layer, part, d, m_axis=0):
    def index(*ids):
        return (layer * MOD_ROWS + _seq_index(ids[m_axis] * tm, lay), 0, part)
    return pl.BlockSpec((1, 1, d), index)


def _norm_modulate(x, gain, scale, shift):
    ms = jnp.mean(x * x, axis=-1, keepdims=True)
    y = x * lax.rsqrt(ms + EPS) * gain
    return y * (1.0 + scale) + shift


def _in_proj_body(x_ref, sh_ref, sc_ref, gain_ref, w_ref, o_ref):
    h = _norm_modulate(x_ref[...], gain_ref[...], sc_ref[0], sh_ref[0])
    o_ref[...] = jnp.dot(h.astype(BF16), w_ref[...], preferred_element_type=F32).astype(o_ref.dtype)


def _in_proj(x, mod3, gain, w, lay, layer, out_dtype):
    t, d = x.shape
    n = w.shape[1]
    tm = _row_tile(lay, 512)
    tn = _tile(n, n // 2 if n % (2 * V7X_MXU_COLS) == 0 else n, V7X_LANES)
    return pl.pallas_call(
        _in_proj_body,
        out_shape=jax.ShapeDtypeStruct((t, n), out_dtype),
        grid=(n // tn, t // tm),
        in_specs=[
            pl.BlockSpec((tm, d), lambda j, m: (m, 0)),
            _mod_spec(lay, tm, layer, 0, d, m_axis=1),
            _mod_spec(lay, tm, layer, 1, d, m_axis=1),
            pl.BlockSpec((1, d), lambda j, m: (0, 0)),
            pl.BlockSpec((d, tn), lambda j, m: (0, j), pipeline_mode=pl.Buffered(1)),
        ],
        out_specs=pl.BlockSpec((tm, tn), lambda j, m: (m, j)),
        compiler_params=_params("parallel", "parallel"),
        name="mixer_in_proj",
    )(x, mod3, mod3, gain.reshape(1, d), w)


def _gates(z, one_minus_lb):
    t = jnp.exp2(jnp.abs(z) * (-LOG2_E))
    r = 1.0 / (1.0 + t)
    k = one_minus_lb * jnp.where(z >= 0, t * r, r)
    log2f = jnp.log2(jnp.maximum(1.0 - k, MIN_FORGET))
    return log2f, k


def _hg_levels(c, leaf):
    widths = []
    w = leaf
    while w < c:
        widths.append(w)
        w *= 2
    return widths


def _hg_wide_constants(c, leaf, reverse):
    t = np.arange(c)[:, None]
    s = np.arange(c)[None, :]
    if reverse:
        t, s = c - 1 - t, c - 1 - s
    cum = (s <= t).astype(np.float32)
    level = np.full((c, c), -1, np.int32)
    level[(s <= t) & (t // leaf == s // leaf)] = 0
    for i, w in enumerate(_hg_levels(c, leaf)):
        level[(s <= t) & (t // (2 * w) == s // (2 * w)) & (t // w != s // w)] = i + 1
    n_levels = 1 + len(_hg_levels(c, leaf))
    wide = np.full((c, n_levels * c), -1, np.int32)
    for l in range(n_levels):
        wide[:, l * c:(l + 1) * c] = np.where(level == l, l, -1)
    return cum, wide, n_levels


def _hg_reference_rows(e_cum, leaf, reverse):
    c, width = e_cum.shape
    mid = leaf // 2 if reverse else leaf // 2 - 1
    refs = [jnp.concatenate(
        [jnp.broadcast_to(e_cum[b + mid:b + mid + 1, :], (leaf, width)) for b in range(0, c, leaf)], axis=0)]
    for w in _hg_levels(c, leaf):
        row = w if reverse else w - 1
        refs.append(jnp.concatenate(
            [jnp.broadcast_to(e_cum[p + row:p + row + 1, :], (2 * w, width)) for p in range(0, c, 2 * w)],
            axis=0))
    return refs


def _hgrn_wide_body(lay, nb, n_levels, n_g, reverse, *refs):
    if reverse:
        q_ref, z_ref, v_ref, g_ref, fwd_ref, lb_ref, gain_ref, cum_ref, wl_ref, o_ref, state_scr = refs
    else:
        q_ref, z_ref, v_ref, lb_ref, cum_ref, wl_ref, o_ref, state_scr = refs
    step = pl.program_id(1)
    c = HG_CHUNK
    n_chunks = q_ref.shape[0] // c
    width = n_g * HEAD_DIM
    blk = (nb - 1 - step) if reverse else step
    row0 = blk * (n_chunks * c)
    bounds = [st + ln for st, ln in zip(lay.starts, lay.lens)] if reverse else list(lay.starts)
    edge = row0 + n_chunks * c if reverse else row0
    reset = functools.reduce(jnp.logical_or, [edge == b for b in bounds])

    @pl.when(reset)
    def _():
        state_scr[...] = jnp.zeros_like(state_scr)

    nt = (((1,), (1,)), ((), ()))
    tn = (((0,), (0,)), ((), ()))
    last_row = 0 if reverse else c - 1

    def chunk(j, carry):
        jj = (n_chunks - 1 - j) if reverse else j
        rows = pl.ds(pl.multiple_of(jj * c, c), c)
        q = q_ref[rows, :].astype(F32)
        v16 = v_ref[rows, :].astype(BF16)
        states = [state_scr[h] for h in range(n_g)]
        log2f, k = _gates(z_ref[rows, :].astype(F32), 1.0 - lb_ref[...])
        hi = log2f.astype(BF16)
        lo = (log2f - hi.astype(F32)).astype(BF16)
        e2 = jnp.dot(cum_ref[...], jnp.concatenate([hi, lo], axis=1), preferred_element_type=F32)
        e_cum = e2[:, :width] + e2[:, width:]
        e_last = e_cum[last_row:last_row + 1, :]
        refs_l = _hg_reference_rows(e_cum, HG_LEAF, reverse)
        e_leaf = e_cum - refs_l[0]
        q_parts = [q * jnp.exp2(e_leaf)]
        k_parts = [k * jnp.exp2(-e_leaf)]
        for ref in refs_l[1:]:
            x = jnp.exp2(-jnp.abs(e_cum - ref))
            q_parts.append(q * x)
            k_parts.append(k * x)
        q_in = (q * jnp.exp2(e_cum)).astype(BF16)
        k_out = (k * jnp.exp2(e_last - e_cum)).astype(BF16)
        state_decay = jnp.exp2(e_last)
        wide_level = wl_ref[...]
        lane_blocks = [slice(b, b + V7X_LANES) for b in range(0, n_levels * c, V7X_LANES)]
        masks = [[(wide_level[:, lb_] == l) for l in range(lb_.start // c, lb_.stop // c)] for lb_ in lane_blocks]
        outs = []
        for h in range(n_g):
            cols = slice(h * HEAD_DIM, (h + 1) * HEAD_DIM)
            q_stack = jnp.concatenate([p[:, cols] for p in q_parts], axis=0).astype(BF16)
            k_stack = jnp.concatenate([p[:, cols] for p in k_parts], axis=0).astype(BF16)
            r = lax.dot_general(q_stack, k_stack, nt, preferred_element_type=F32)
            pieces = []
            for lb_, lane_masks in zip(lane_blocks, masks):
                piece = jnp.zeros((c, V7X_LANES), F32)
                for l, mask in zip(range(lb_.start // c, lb_.stop // c), lane_masks):
                    piece = jnp.where(mask, r[l * c:(l + 1) * c, lb_], piece)
                pieces.append(piece)
            scores = jnp.concatenate(pieces, axis=1)
            v_h = v16[:, cols]
            o = jnp.dot(scores.astype(BF16), jnp.concatenate([v_h] * n_levels, axis=0),
                        preferred_element_type=F32)
            o = o + lax.dot_general(q_in[:, cols], states[h].astype(BF16), nt, preferred_element_type=F32)
            upd = lax.dot_general(v_h, k_out[:, cols], tn, preferred_element_type=F32)
            states[h] = states[h] * state_decay[:, cols] + upd
            outs.append(o)
        for h in range(n_g):
            state_scr[h] = states[h]
        if reverse:
            g = g_ref[rows, :].astype(F32)
            fwd = fwd_ref[rows, :]
            for h in range(n_g):
                cols = slice(h * HEAD_DIM, (h + 1) * HEAD_DIM)
                tot = fwd[:, cols] + outs[h]
                ms = jnp.mean(tot * tot, axis=-1, keepdims=True)
                y = tot * lax.rsqrt(ms + EPS) * gain_ref[...]
                o_ref[rows, cols] = (y * _silu(g[:, cols])).astype(o_ref.dtype)
        else:
            for h in range(n_g):
                o_ref[rows, h * HEAD_DIM:(h + 1) * HEAD_DIM] = outs[h]
        return carry

    lax.fori_loop(0, n_chunks, chunk, 0, unroll=HG_UNROLL)


def _hgrn_dir(proj, z_group, lb, fwd_out, out_gain, lay, n_heads, reverse, out_dtype):
    t = proj.shape[0]
    lb_rows = _tile(int(np.gcd.reduce(lay.lens)), HG_BLOCK, HG_CHUNK)
    nb = t // lb_rows
    n_g = min(n_heads, HG_HEADS_PER_STEP)
    assert n_heads % n_g == 0
    n_groups = n_heads // n_g
    mats, level, n_levels = _hg_wide_constants(HG_CHUNK, HG_LEAF, reverse)
    width = n_g * HEAD_DIM

    def col(group):
        return lambda g, s: ((nb - 1 - s) if reverse else s, group * n_groups + g)

    blk = (lb_rows, width)
    const = lambda g, s: (0, 0)
    in_specs = [pl.BlockSpec(blk, col(0)), pl.BlockSpec(blk, col(z_group)), pl.BlockSpec(blk, col(3))]
    args = [proj, proj, proj]
    if reverse:
        in_specs += [pl.BlockSpec(blk, col(4)), pl.BlockSpec(blk, col(0))]
        args += [proj, fwd_out]
    in_specs.append(pl.BlockSpec((1, width), lambda g, s: (0, g)))
    args.append(lb.reshape(1, n_heads * HEAD_DIM))
    if reverse:
        in_specs.append(pl.BlockSpec((1, HEAD_DIM), const))
        args.append(out_gain.reshape(1, HEAD_DIM))
    in_specs += [pl.BlockSpec(mats.shape, const), pl.BlockSpec(level.shape, const)]
    args += [jnp.asarray(mats, BF16), jnp.asarray(level)]
    return pl.pallas_call(
        functools.partial(_hgrn_wide_body, lay, nb, n_levels, n_g, reverse),
        out_shape=jax.ShapeDtypeStruct((t, n_heads * HEAD_DIM), out_dtype),
        grid=(n_groups, nb),
        in_specs=in_specs,
        out_specs=pl.BlockSpec(blk, col(0)),
        scratch_shapes=[pltpu.VMEM((n_g, HEAD_DIM, HEAD_DIM), F32)],
        compiler_params=_params("parallel", "arbitrary"),
        name="hgrn2_bwd" if reverse else "hgrn2_fwd",
    )(*args)


def _hgrn(proj, lbs_f, lbs_b, out_gain, lay, n_heads, out_dtype):
    o_fwd = _hgrn_dir(proj, 1, lbs_f, None, None, lay, n_heads, False, F32)
    return _hgrn_dir(proj, 2, lbs_b, o_fwd, out_gain, lay, n_heads, True, out_dtype)


def _rope_tables(n_pos):
    n_freq = HEAD_DIM // 4
    pos = jnp.arange(n_pos)
    coords = jnp.stack([pos // GRID_W, pos % GRID_W], axis=-1).astype(F32)
    inv = ROPE_THETA ** (-jnp.arange(n_freq, dtype=F32) / n_freq)
    ang = coords[:, :, None] * inv
    cos = jnp.broadcast_to(jnp.cos(ang)[:, :, None, :], (n_pos, 2, 2, n_freq))
    sin = jnp.sin(ang)
    sin = jnp.stack([-sin, sin], axis=2)
    return cos.reshape(n_pos, HEAD_DIM), sin.reshape(n_pos, HEAD_DIM)


def _qk_prep_body(n_q, n_kv, q_ref, k_ref, v_ref, cos_ref, sin_ref, qg_ref, kg_ref,
                  qo_ref, ko_ref, vo_ref):
    cos = cos_ref[...]
    sin = sin_ref[...]
    lane = lax.broadcasted_iota(jnp.int32, cos.shape, 1)
    first_half = (lane % (HEAD_DIM // 2)) < (HEAD_DIM // 4)

    def norm_rope(x, gain, scale):
        ms = jnp.mean(x * x, axis=-1, keepdims=True)
        y = x * lax.rsqrt(ms + EPS) * gain
        quarter = HEAD_DIM // 4
        partner = jnp.where(first_half, pltpu.roll(y, HEAD_DIM - quarter, 1), pltpu.roll(y, quarter, 1))
        out = y * cos + partner * sin
        return out * scale if scale is not None else out

    for h in range(n_q):
        cols = slice(h * HEAD_DIM, (h + 1) * HEAD_DIM)
        qo_ref[:, cols] = norm_rope(q_ref[:, cols].astype(F32), qg_ref[...],
                                    HEAD_DIM ** -0.5 * LOG2_E).astype(qo_ref.dtype)
    for h in range(n_kv):
        cols = slice(h * HEAD_DIM, (h + 1) * HEAD_DIM)
        ko_ref[:, cols] = norm_rope(k_ref[:, cols].astype(F32), kg_ref[...], None).astype(ko_ref.dtype)
    vo_ref[...] = v_ref[...].astype(vo_ref.dtype)


def _qk_prep(proj, q_gain, k_gain, cos, sin, lay, q_col, n_q, n_kv):
    t = proj.shape[0]
    tm = _row_tile(lay, 256)
    qw, kw = n_q * HEAD_DIM, n_kv * HEAD_DIM
    assert q_col % qw == 0 and (q_col + qw) % kw == 0

    def pos_block(m):
        row = m * tm
        return ((row - _seq_start(row, lay)) // tm, 0)

    return pl.pallas_call(
        functools.partial(_qk_prep_body, n_q, n_kv),
        out_shape=(jax.ShapeDtypeStruct((t, qw), BF16), jax.ShapeDtypeStruct((t, kw), BF16),
                   jax.ShapeDtypeStruct((t, kw), BF16)),
        grid=(t // tm,),
        in_specs=[
            pl.BlockSpec((tm, qw), lambda m: (m, q_col // qw)),
            pl.BlockSpec((tm, kw), lambda m: (m, (q_col + qw) // kw)),
            pl.BlockSpec((tm, kw), lambda m: (m, (q_col + qw) // kw + 1)),
            pl.BlockSpec((tm, HEAD_DIM), pos_block),
            pl.BlockSpec((tm, HEAD_DIM), pos_block),
            pl.BlockSpec((1, HEAD_DIM), lambda m: (0, 0)),
            pl.BlockSpec((1, HEAD_DIM), lambda m: (0, 0)),
        ],
        out_specs=(pl.BlockSpec((tm, qw), lambda m: (m, 0)), pl.BlockSpec((tm, kw), lambda m: (m, 0)),
                   pl.BlockSpec((tm, kw), lambda m: (m, 0))),
        compiler_params=_params("parallel"),
        name="attn_qk_prep",
    )(proj, proj, proj, cos, sin, q_gain.reshape(1, HEAD_DIM), k_gain.reshape(1, HEAD_DIM))


def _flash_body(qb_ref, kb_ref, flag_ref, q_ref, k_ref, v_ref, o_ref, m_scr, l_scr, acc_scr):
    i = pl.program_id(1)
    flags = flag_ref[i]
    tq = q_ref.shape[0]

    @pl.when((flags & 1) != 0)
    def _():
        m_scr[...] = jnp.full_like(m_scr, -jnp.inf)
        l_scr[...] = jnp.zeros_like(l_scr)
        acc_scr[...] = jnp.zeros_like(acc_scr)

    q = q_ref[...]
    qs = jnp.concatenate([q[:, g * HEAD_DIM:(g + 1) * HEAD_DIM] for g in range(ATT_GROUP)], axis=0)
    s = lax.dot_general(qs, k_ref[...], (((1,), (1,)), ((), ())), preferred_element_type=F32)
    m_prev = m_scr[...]
    m_new = jnp.maximum(m_prev, jnp.max(s, axis=-1, keepdims=True))
    alpha = jnp.exp2(m_prev - m_new)
    p = jnp.exp2(s - m_new)
    l_scr[...] = alpha * l_scr[...] + jnp.sum(p, axis=-1, keepdims=True)
    acc_scr[...] = alpha * acc_scr[...] + jnp.dot(p.astype(BF16), v_ref[...], preferred_element_type=F32)
    m_scr[...] = m_new

    @pl.when((flags & 2) != 0)
    def _():
        out = acc_scr[...] / l_scr[...]
        for g in range(ATT_GROUP):
            o_ref[:, g * HEAD_DIM:(g + 1) * HEAD_DIM] = out[g * tq:(g + 1) * tq].astype(o_ref.dtype)


def _flash_attention(q_rot, k_rot, v16, lay, n_kv, out_dtype):
    t = q_rot.shape[0]
    g = int(np.gcd.reduce(lay.lens))
    tq = _tile(g, 256, 16)
    tk = _tile(g, FLASH_KV_ROWS, V7X_LANES)
    qb, kb, flags = [], [], []
    for st, ln in zip(lay.starts, lay.lens):
        for qi in range(ln // tq):
            for ki in range(ln // tk):
                qb.append(st // tq + qi)
                kb.append(st // tk + ki)
                flags.append((1 if ki == 0 else 0) | (2 if ki == ln // tk - 1 else 0))
    n_items = len(qb)
    gw = ATT_GROUP * HEAD_DIM
    grid_spec = pltpu.PrefetchScalarGridSpec(
        num_scalar_prefetch=3,
        grid=(n_kv, n_items),
        in_specs=[
            pl.BlockSpec((tq, gw), lambda h, i, qb, kb, fl: (qb[i], h)),
            pl.BlockSpec((tk, HEAD_DIM), lambda h, i, qb, kb, fl: (kb[i], h)),
            pl.BlockSpec((tk, HEAD_DIM), lambda h, i, qb, kb, fl: (kb[i], h)),
        ],
        out_specs=pl.BlockSpec((tq, gw), lambda h, i, qb, kb, fl: (qb[i], h)),
        scratch_shapes=[pltpu.VMEM((ATT_GROUP * tq, 1), F32), pltpu.VMEM((ATT_GROUP * tq, 1), F32),
                        pltpu.VMEM((ATT_GROUP * tq, HEAD_DIM), F32)],
    )
    return pl.pallas_call(
        _flash_body,
        out_shape=jax.ShapeDtypeStruct((t, n_kv * gw), out_dtype),
        grid_spec=grid_spec,
        compiler_params=_params("parallel", "arbitrary"),
        name="attn_flash",
    )(jnp.asarray(qb, jnp.int32), jnp.asarray(kb, jnp.int32), jnp.asarray(flags, jnp.int32),
      q_rot, k_rot, v16)


def _out_proj_body(a_ref, b_ref, wa_ref, wb_ref, x_ref, g_ref, o_ref):
    acc = jnp.dot(a_ref[...], wa_ref[...], preferred_element_type=F32)
    acc = acc + jnp.dot(b_ref[...], wb_ref[...], preferred_element_type=F32)
    o_ref[...] = x_ref[...] + g_ref[0] * acc


def _out_proj(o_hg, o_att, w_out, x, mod3, lay, layer):
    t, d = x.shape
    ka, kb = o_hg.shape[1], o_att.shape[1]
    tm = _row_tile(lay, 512)
    return pl.pallas_call(
        _out_proj_body,
        out_shape=jax.ShapeDtypeStruct((t, d), F32),
        grid=(t // tm,),
        in_specs=[
            pl.BlockSpec((tm, ka), lambda m: (m, 0)),
            pl.BlockSpec((tm, kb), lambda m: (m, 0)),
            pl.BlockSpec((ka, d), lambda m: (0, 0)),
            pl.BlockSpec((kb, d), lambda m: (1, 0)),
            pl.BlockSpec((tm, d), lambda m: (m, 0)),
            _mod_spec(lay, tm, layer, 2, d),
        ],
        out_specs=pl.BlockSpec((tm, d), lambda m: (m, 0)),
        compiler_params=_params("parallel"),
        name="mixer_out_proj",
    )(o_hg, o_att, w_out, w_out, x, mod3)


def _ffn_in_body(x_ref, sh_ref, sc_ref, gain_ref, o_ref):
    o_ref[...] = _norm_modulate(x_ref[...], gain_ref[...], sc_ref[0], sh_ref[0]).astype(o_ref.dtype)


def _router_body(x_ref, sh_ref, sc_ref, gain_ref, wr_ref, o_ref, idx_ref, gate_ref):
    h = _norm_modulate(x_ref[...], gain_ref[...], sc_ref[0], sh_ref[0])
    o_ref[...] = h.astype(o_ref.dtype)
    logits = jnp.dot(h, wr_ref[...], preferred_element_type=F32, precision=lax.Precision.HIGHEST)
    n_e = logits.shape[1]
    lane = lax.broadcasted_iota(jnp.int32, logits.shape, 1)
    m1 = jnp.max(logits, axis=-1, keepdims=True)
    i1 = jnp.min(jnp.where(logits == m1, lane, n_e), axis=-1, keepdims=True)
    rest = jnp.where(lane == i1, -jnp.inf, logits)
    m2 = jnp.max(rest, axis=-1, keepdims=True)
    i2 = jnp.min(jnp.where(rest == m2, lane, n_e), axis=-1, keepdims=True)
    e = jnp.exp(m2 - m1)
    tot = 1.0 + e
    two = lax.broadcasted_iota(jnp.int32, idx_ref.shape, 1)
    idx_ref[...] = jnp.where(two == 0, i1, i2)
    gate_ref[...] = jnp.where(two == 0, 1.0 / tot, e / tot)


def _ffn_in(x, mod3, gain, lay, layer, w_router=None):
    t, d = x.shape
    tm = _row_tile(lay, 256)
    in_specs = [
        pl.BlockSpec((tm, d), lambda m: (m, 0)),
        _mod_spec(lay, tm, layer, 3, d),
        _mod_spec(lay, tm, layer, 4, d),
        pl.BlockSpec((1, d), lambda m: (0, 0)),
    ]
    row_spec = pl.BlockSpec((tm, d), lambda m: (m, 0))
    if w_router is None:
        return pl.pallas_call(
            _ffn_in_body,
            out_shape=jax.ShapeDtypeStruct((t, d), BF16),
            grid=(t // tm,), in_specs=in_specs, out_specs=row_spec,
            compiler_params=_params("parallel"), name="ffn_in",
        )(x, mod3, mod3, gain.reshape(1, d))
    n_e = w_router.shape[1]
    small = pl.BlockSpec((tm, TOP_K), lambda m: (m, 0))
    return pl.pallas_call(
        _router_body,
        out_shape=(jax.ShapeDtypeStruct((t, d), F32), jax.ShapeDtypeStruct((t, TOP_K), jnp.int32),
                   jax.ShapeDtypeStruct((t, TOP_K), F32)),
        grid=(t // tm,),
        in_specs=in_specs + [pl.BlockSpec((d, n_e), lambda m: (0, 0))],
        out_specs=(row_spec, small, small),
        compiler_params=_params("parallel"), name="moe_router",
    )(x, mod3, mod3, gain.reshape(1, d), w_router)


def _swiglu_step(x16, wg_ref, wu_ref, wd_ref, acc_scr):
    gate = jnp.dot(x16, wg_ref[0], preferred_element_type=F32)
    up = jnp.dot(x16, wu_ref[0], preferred_element_type=F32)
    act = (_silu(gate) * up).astype(BF16)
    acc_scr[...] += jnp.dot(act, wd_ref[0], preferred_element_type=F32)


def _dense_ffn_body(x_ref, wg_ref, wu_ref, wd_ref, res_ref, g_ref, o_ref, acc_scr):
    f = pl.program_id(1)

    @pl.when(f == 0)
    def _():
        acc_scr[...] = jnp.zeros_like(acc_scr)

    _swiglu_step(x_ref[...], wg_ref, wu_ref, wd_ref, acc_scr)

    @pl.when(f == pl.num_programs(1) - 1)
    def _():
        o_ref[...] = res_ref[...] + g_ref[0] * acc_scr[...]


def _dense_ffn(h16, w_gate, w_up, w_down, j, x, mod3, lay, layer):
    t, d = x.shape
    fdim = w_gate.shape[2]
    tm = _row_tile(lay, 512)
    tf = _tile(fdim, 512, V7X_LANES)
    return pl.pallas_call(
        _dense_ffn_body,
        out_shape=jax.ShapeDtypeStruct((t, d), F32),
        grid=(t // tm, fdim // tf),
        in_specs=[
            pl.BlockSpec((tm, d), lambda m, f: (m, 0)),
            pl.BlockSpec((1, d, tf), lambda m, f: (j, 0, f)),
            pl.BlockSpec((1, d, tf), lambda m, f: (j, 0, f)),
            pl.BlockSpec((1, tf, d), lambda m, f: (j, f, 0)),
            pl.BlockSpec((tm, d), lambda m, f: (m, 0)),
            _mod_spec(lay, tm, layer, 5, d),
        ],
        out_specs=pl.BlockSpec((tm, d), lambda m, f: (m, 0)),
        scratch_shapes=[pltpu.VMEM((tm, d), F32)],
        compiler_params=_params("parallel", "arbitrary"),
        name="swiglu_residual",
    )(h16, w_gate, w_up, w_down, x, mod3)


def _expert_ffn_body(be_ref, nu_ref, tok_ref, h_hbm, wg_ref, wu_ref, wd_ref, o_ref,
                     xbuf, x16_scr, acc_scr, sems):
    m = pl.program_id(0)
    f = pl.program_id(1)
    tm = x16_scr.shape[0]
    n_used = nu_ref[0]
    used = m < n_used
    last = f == pl.num_programs(1) - 1

    def row_copy(src_row, slot, dst_row):
        return pltpu.make_async_copy(h_hbm.at[pl.ds(src_row, 1)], xbuf.at[slot, pl.ds(dst_row, 1)],
                                     sems.at[slot])

    def gather(block, slot):
        def issue(r, carry):
            row_copy(tok_ref[block * tm + r], slot, r).start()
            return carry
        lax.fori_loop(0, tm, issue, 0)

    def wait_rows(slot):
        def drain(r, carry):
            row_copy(0, slot, r).wait()
            return carry
        lax.fori_loop(0, tm, drain, 0)

    @pl.when(jnp.logical_and(used, f == 0))
    def _():
        slot = m % 2

        @pl.when(m == 0)
        def _():
            gather(0, 0)

        wait_rows(slot)
        x16_scr[...] = xbuf[slot].astype(BF16)
        acc_scr[...] = jnp.zeros_like(acc_scr)

        @pl.when(m + 1 < n_used)
        def _():
            gather(m + 1, 1 - slot)

    @pl.when(used)
    def _():
        _swiglu_step(x16_scr[...], wg_ref, wu_ref, wd_ref, acc_scr)

    @pl.when(jnp.logical_and(used, last))
    def _():
        o_ref[...] = acc_scr[...]

    @pl.when(jnp.logical_and(jnp.logical_not(used), last))
    def _():
        o_ref[...] = jnp.zeros_like(o_ref)


def _expert_ffn(h32, row_token, w_gate, w_up, w_down, block_expert, n_used, tm):
    d = h32.shape[1]
    r = row_token.shape[0]
    fdim = w_gate.shape[2]
    tf = _tile(fdim, 512, V7X_LANES)
    nf = fdim // tf

    def fcol(m, f, nu):
        return jnp.where(m < nu[0], f, nf - 1)

    grid_spec = pltpu.PrefetchScalarGridSpec(
        num_scalar_prefetch=3,
        grid=(r // tm, nf),
        in_specs=[
            pl.BlockSpec(memory_space=pl.ANY),
            pl.BlockSpec((1, d, tf), lambda m, f, be, nu, tok: (be[m], 0, fcol(m, f, nu))),
            pl.BlockSpec((1, d, tf), lambda m, f, be, nu, tok: (be[m], 0, fcol(m, f, nu))),
            pl.BlockSpec((1, tf, d), lambda m, f, be, nu, tok: (be[m], fcol(m, f, nu), 0)),
        ],
        out_specs=pl.BlockSpec((tm, d), lambda m, f, be, nu, tok: (m, 0)),
        scratch_shapes=[pltpu.VMEM((2, tm, d), F32), pltpu.VMEM((tm, d), BF16), pltpu.VMEM((tm, d), F32),
                        pltpu.SemaphoreType.DMA((2,))],
    )
    return pl.pallas_call(
        _expert_ffn_body,
        out_shape=jax.ShapeDtypeStruct((r, d), F32),
        grid_spec=grid_spec,
        compiler_params=_params("arbitrary", "arbitrary"),
        name="swiglu_experts",
    )(block_expert, n_used, row_token, h32, w_gate, w_up, w_down)


def _combine_body(dest_ref, yb_hbm, x_ref, gate_ref, g_ref, o_ref, buf, sem):
    tm = x_ref.shape[0]
    base = pl.program_id(0) * tm

    def row_copy(src_row, k, dst_row):
        return pltpu.make_async_copy(yb_hbm.at[pl.ds(src_row, 1)], buf.at[k, pl.ds(dst_row, 1)], sem)

    def issue(r, carry):
        for k in range(TOP_K):
            row_copy(dest_ref[(base + r) * TOP_K + k], k, r).start()
        return carry

    def drain(r, carry):
        for k in range(TOP_K):
            row_copy(0, k, r).wait()
        return carry

    lax.fori_loop(0, tm, issue, 0)
    lax.fori_loop(0, tm, drain, 0)
    gates = gate_ref[...]
    y = buf[0] * gates[:, 0:1]
    for k in range(1, TOP_K):
        y = y + buf[k] * gates[:, k:k + 1]
    o_ref[...] = x_ref[...] + g_ref[0] * y


def _combine(yb, dest, gates, x, mod3, lay, layer):
    t, d = x.shape
    tm = _row_tile(lay, 256)
    grid_spec = pltpu.PrefetchScalarGridSpec(
        num_scalar_prefetch=1,
        grid=(t // tm,),
        in_specs=[
            pl.BlockSpec(memory_space=pl.ANY),
            pl.BlockSpec((tm, d), lambda m, dest: (m, 0)),
            pl.BlockSpec((tm, TOP_K), lambda m, dest: (m, 0)),
            _mod_spec(lay, tm, layer, 5, d),
        ],
        out_specs=pl.BlockSpec((tm, d), lambda m, dest: (m, 0)),
        scratch_shapes=[pltpu.VMEM((TOP_K, tm, d), yb.dtype), pltpu.SemaphoreType.DMA],
    )
    return pl.pallas_call(
        _combine_body,
        out_shape=jax.ShapeDtypeStruct((t, d), F32),
        grid_spec=grid_spec,
        compiler_params=_params("arbitrary"),
        name="moe_combine",
    )(dest, yb, x, gates, mod3)


def _moe(x, h32, idx, gates, w_gate, w_up, w_down, first_expert, n_e, mod3, lay, layer):
    t, d = x.shape
    tm = _row_tile(lay, 512)
    n_slots = t * TOP_K
    cap = n_slots + n_e * tm
    e = idx.reshape(-1)
    onehot = (e[:, None] == jnp.arange(n_e, dtype=jnp.int32)[None, :]).astype(jnp.int32)
    csum = jnp.cumsum(onehot, axis=0)
    rank = jnp.take_along_axis(csum, e[:, None], axis=1)[:, 0] - 1
    counts = csum[-1]
    padded = (counts + tm - 1) // tm * tm
    pad_end = jnp.cumsum(padded)
    pad_start = pad_end - padded
    dest = (pad_start[e] + rank).astype(jnp.int32)
    row_token = jnp.zeros((cap,), jnp.int32).at[dest].set(jnp.arange(n_slots, dtype=jnp.int32) // TOP_K)
    block_first = jnp.arange(cap // tm, dtype=jnp.int32) * tm
    block_e = jnp.minimum(jnp.searchsorted(pad_end, block_first, side='right'), n_e - 1).astype(jnp.int32)
    n_used = (pad_end[-1:] // tm).astype(jnp.int32)
    block_e = jnp.where(block_first < pad_end[-1], block_e, block_e[jnp.maximum(n_used[0] - 1, 0)])
    yb = _expert_ffn(h32, row_token, w_gate, w_up, w_down, block_e + first_expert, n_used, tm)
    return _combine(yb, dest, gates, x, mod3, lay, layer)


def _final_norm_body(x_ref, gain_ref, o_ref):
    x = x_ref[...]
    ms = jnp.mean(x * x, axis=-1, keepdims=True)
    o_ref[...] = x * lax.rsqrt(ms + EPS) * gain_ref[...]


def _final_norm(x, gain, row0, n_rows, tm):
    d = x.shape[1]
    first = row0 // tm
    return pl.pallas_call(
        _final_norm_body,
        out_shape=jax.ShapeDtypeStruct((n_rows, d), F32),
        grid=(n_rows // tm,),
        in_specs=[pl.BlockSpec((tm, d), lambda m: (first + m, 0)), pl.BlockSpec((1, d), lambda m: (0, 0))],
        out_specs=pl.BlockSpec((tm, d), lambda m: (m, 0)),
        compiler_params=_params("parallel"),
        name="final_norm",
    )(x, gain.reshape(1, d))


def kernel(x_prompt, x_sample, c_prompt, c_sample, w_mod, b_mod, norm_mix, norm_ffn, w_in, hg_lb_logits,
           hg_out_norm, qk_norm, w_out, ffn_gate, ffn_up, ffn_down, router, exp_gate, exp_up, exp_down,
           final_norm):
    depth, d = norm_mix.shape
    lay = _layout(x_prompt, x_sample)
    n_seq = len(lay.starts)
    assert n_seq <= MOD_ROWS
    hg_heads = hg_lb_logits.shape[2] // HEAD_DIM
    kv_heads = (w_in.shape[2] - 5 * hg_heads * HEAD_DIM) // HEAD_DIM // (ATT_GROUP + 2)
    att_heads = kv_heads * ATT_GROUP
    hg_width = hg_heads * HEAD_DIM
    n_e = exp_gate.shape[1]

    x = jnp.concatenate([x_prompt.reshape(-1, d), x_sample.reshape(-1, d)], axis=0)
    c_rows = jnp.concatenate([c_prompt, c_sample, jnp.zeros((MOD_ROWS - n_seq, d), F32)], axis=0)
    mod3 = _modulation(c_rows, w_mod, b_mod).reshape(depth * MOD_ROWS, 1, 6 * d)
    lbs = _lower_bounds(hg_lb_logits)
    cos, sin = _rope_tables(max(lay.lens))

    w_in16 = w_in.astype(BF16)
    w_out16 = w_out.astype(BF16)
    dense16 = [w.astype(BF16) for w in (ffn_gate, ffn_up, ffn_down)]
    expert16 = [w.astype(BF16).reshape((-1,) + w.shape[2:]) for w in (exp_gate, exp_up, exp_down)]

    for layer in range(depth):
        proj = _in_proj(x, mod3, norm_mix[layer], w_in16[layer], lay, layer, PROJ_DTYPE)
        o_hg = _hgrn(proj, lbs[0, layer], lbs[1, layer], hg_out_norm[layer], lay, hg_heads, BF16)
        q_rot, k_rot, v16 = _qk_prep(proj, qk_norm[layer, 0], qk_norm[layer, 1], cos, sin, lay,
                                     5 * hg_width, att_heads, kv_heads)
        o_att = _flash_attention(q_rot, k_rot, v16, lay, kv_heads, BF16)
        x = _out_proj(o_hg, o_att, w_out16[layer], x, mod3, lay, layer)
        j = layer // 2
        if layer % 2 == 0:
            h16 = _ffn_in(x, mod3, norm_ffn[layer], lay, layer)
            x = _dense_ffn(h16, *dense16, j, x, mod3, lay, layer)
        else:
            h32, idx, gates = _ffn_in(x, mod3, norm_ffn[layer], lay, layer, w_router=router[j])
            x = _moe(x, h32, idx, gates, *expert16, j * n_e, n_e, mod3, lay, layer)

    tm = _row_tile(lay, 512)
    n_prompt = x_prompt.shape[0] * x_prompt.shape[1]
    y_prompt = _final_norm(x, final_norm, 0, n_prompt, tm).reshape(x_prompt.shape)
    y_sample = _final_norm(x, final_norm, n_prompt, lay.total - n_prompt, tm).reshape(x_sample.shape)
    return (y_prompt, y_sample)
```

```python
import functools
from typing import NamedTuple

import numpy as np
import jax
import jax.numpy as jnp
from jax import lax
from jax.experimental import pallas as pl
from jax.experimental.pallas import tpu as pltpu

F32 = jnp.float32
BF16 = jnp.bfloat16

EPS = 1e-6
MIN_FORGET = 1e-6
GRID_W = 64
ROPE_THETA = 10000.0
HEAD_DIM = 128
ATT_GROUP = 4
TOP_K = 2
LOG2_E = 1.4426950408889634

V7X_LANES = 128
V7X_MXU_COLS = 256
V7X_VMEM_LIMIT_BYTES = 56 * 1024 * 1024

FLASH_KV_ROWS = 4096
HG_CHUNK = 64
HG_LEAF = 8
HG_BLOCK = 512
HG_HEADS_PER_STEP = 8
HG_UNROLL = 4
MOD_ROWS = 8
PROJ_DTYPE = BF16


class Layout(NamedTuple):
    starts: tuple
    lens: tuple
    total: int


def _layout(x_prompt, x_sample):
    starts, lens, row = [], [], 0
    for arr in (x_prompt, x_sample):
        for _ in range(arr.shape[0]):
            starts.append(row)
            lens.append(arr.shape[1])
            row += arr.shape[1]
    return Layout(tuple(starts), tuple(lens), row)


def _seq_index(row, lay):
    s = jnp.int32(0)
    for st in lay.starts[1:]:
        s = s + (row >= st).astype(jnp.int32)
    return s


def _seq_start(row, lay):
    s = jnp.int32(0)
    for st in lay.starts[1:]:
        s = jnp.where(row >= st, jnp.int32(st), s)
    return s


def _tile(n, target, align):
    best = None
    t = align
    while t <= min(n, target):
        if n % t == 0:
            best = t
        t += align
    assert best is not None, (n, target, align)
    return best


def _row_tile(lay, target):
    return _tile(int(np.gcd.reduce(lay.lens)), target, 16)


def _params(*sem):
    return pltpu.CompilerParams(dimension_semantics=sem, vmem_limit_bytes=V7X_VMEM_LIMIT_BYTES)


def _silu(x):
    return x * (1.0 / (1.0 + jnp.exp(-x)))


def _lower_bounds_body(depth, l_ref, o_ref):
    for d in range(2):
        rows = [l_ref[d * depth + i:d * depth + i + 1, :] for i in range(depth)]
        m = functools.reduce(jnp.maximum, rows)
        e = [jnp.exp(r - m) for r in rows]
        tot = functools.reduce(lambda a, b: a + b, e)
        s = [ei / tot for ei in e]
        run = None
        for i in range(depth):
            run = s[i] if run is None else run + s[i]
            o_ref[d * depth + i:d * depth + i + 1, :] = run - s[0]


def _lower_bounds(lb_logits):
    two, depth, width = lb_logits.shape
    out = pl.pallas_call(
        functools.partial(_lower_bounds_body, depth),
        out_shape=jax.ShapeDtypeStruct((two * depth, width), F32),
        name="hgrn_lower_bounds",
    )(lb_logits.reshape(two * depth, width).astype(F32))
    return out.reshape(two, depth, width)


def _modulation_body(c_ref, w_ref, b_ref, o_ref):
    sc = _silu(c_ref[...]).astype(BF16)
    acc = jnp.dot(sc, w_ref[0].astype(BF16), preferred_element_type=F32)
    o_ref[0] = acc + b_ref[0]


def _modulation(c_rows, w_mod, b_mod):
    depth, d, n = w_mod.shape
    rows = c_rows.shape[0]
    tn = _tile(n, 1024, V7X_LANES)
    return pl.pallas_call(
        _modulation_body,
        out_shape=jax.ShapeDtypeStruct((depth, rows, n), F32),
        grid=(depth, n // tn),
        in_specs=[
            pl.BlockSpec((rows, d), lambda l, j: (0, 0)),
            pl.BlockSpec((1, d, tn), lambda l, j: (l, 0, j)),
            pl.BlockSpec((1, 1, tn), lambda l, j: (l, 0, j)),
        ],
        out_specs=pl.BlockSpec((1, rows, tn), lambda l, j: (l, 0, j)),
        compiler_params=_params("parallel", "parallel"),
        name="adaln_modulation",
    )(c_rows, w_mod, b_mod.reshape(depth, 1, n))


def _mod_spec(lay, tm, layer, part, d, m_axis=0):
    def index(*ids):
        return (layer * MOD_ROWS + _seq_index(ids[m_axis] * tm, lay), 0, part)
    return pl.BlockSpec((1, 1, d), index)


def _norm_modulate(x, gain, scale, shift):
    ms = jnp.mean(x * x, axis=-1, keepdims=True)
    y = x * lax.rsqrt(ms + EPS) * gain
    return y * (1.0 + scale) + shift


def _in_proj_body(x_ref, sh_ref, sc_ref, gain_ref, w_ref, o_ref):
    h = _norm_modulate(x_ref[...], gain_ref[...], sc_ref[0], sh_ref[0])
    o_ref[...] = jnp.dot(h.astype(BF16), w_ref[...], preferred_element_type=F32).astype(o_ref.dtype)


def _in_proj(x, mod3, gain, w, lay, layer, out_dtype):
    t, d = x.shape
    n = w.shape[1]
    tm = _row_tile(lay, 512)
    tn = _tile(n, n // 2 if n % (2 * V7X_MXU_COLS) == 0 else n, V7X_LANES)
    return pl.pallas_call(
        _in_proj_body,
        out_shape=jax.ShapeDtypeStruct((t, n), out_dtype),
        grid=(n // tn, t // tm),
        in_specs=[
            pl.BlockSpec((tm, d), lambda j, m: (m, 0)),
            _mod_spec(lay, tm, layer, 0, d, m_axis=1),
            _mod_spec(lay, tm, layer, 1, d, m_axis=1),
            pl.BlockSpec((1, d), lambda j, m: (0, 0)),
            pl.BlockSpec((d, tn), lambda j, m: (0, j), pipeline_mode=pl.Buffered(1)),
        ],
        out_specs=pl.BlockSpec((tm, tn), lambda j, m: (m, j)),
        compiler_params=_params("parallel", "parallel"),
        name="mixer_in_proj",
    )(x, mod3, mod3, gain.reshape(1, d), w)


def _gates(z, one_minus_lb):
    t = jnp.exp2(jnp.abs(z) * (-LOG2_E))
    r = 1.0 / (1.0 + t)
    k = one_minus_lb * jnp.where(z >= 0, t * r, r)
    log2f = jnp.log2(jnp.maximum(1.0 - k, MIN_FORGET))
    return log2f, k


def _hg_levels(c, leaf):
    widths = []
    w = leaf
    while w < c:
        widths.append(w)
        w *= 2
    return widths


def _hg_wide_constants(c, leaf, reverse):
    t = np.arange(c)[:, None]
    s = np.arange(c)[None, :]
    if reverse:
        t, s = c - 1 - t, c - 1 - s
    cum = (s <= t).astype(np.float32)
    level = np.full((c, c), -1, np.int32)
    level[(s <= t) & (t // leaf == s // leaf)] = 0
    for i, w in enumerate(_hg_levels(c, leaf)):
        level[(s <= t) & (t // (2 * w) == s // (2 * w)) & (t // w != s // w)] = i + 1
    n_levels = 1 + len(_hg_levels(c, leaf))
    wide = np.full((c, n_levels * c), -1, np.int32)
    for l in range(n_levels):
        wide[:, l * c:(l + 1) * c] = np.where(level == l, l, -1)
    return cum, wide, n_levels


def _hg_reference_rows(e_cum, leaf, reverse):
    c, width = e_cum.shape
    mid = leaf // 2 if reverse else leaf // 2 - 1
    refs = [jnp.concatenate(
        [jnp.broadcast_to(e_cum[b + mid:b + mid + 1, :], (leaf, width)) for b in range(0, c, leaf)], axis=0)]
    for w in _hg_levels(c, leaf):
        row = w if reverse else w - 1
        refs.append(jnp.concatenate(
            [jnp.broadcast_to(e_cum[p + row:p + row + 1, :], (2 * w, width)) for p in range(0, c, 2 * w)],
            axis=0))
    return refs


def _hgrn_wide_body(lay, nb, n_levels, n_g, reverse, *refs):
    if reverse:
        q_ref, z_ref, v_ref, g_ref, fwd_ref, lb_ref, gain_ref, cum_ref, wl_ref, o_ref, state_scr = refs
    else:
        q_ref, z_ref, v_ref, lb_ref, cum_ref, wl_ref, o_ref, state_scr = refs
    step = pl.program_id(1)
    c = HG_CHUNK
    n_chunks = q_ref.shape[0] // c
    width = n_g * HEAD_DIM
    blk = (nb - 1 - step) if reverse else step
    row0 = blk * (n_chunks * c)
    bounds = [st + ln for st, ln in zip(lay.starts, lay.lens)] if reverse else list(lay.starts)
    edge = row0 + n_chunks * c if reverse else row0
    reset = functools.reduce(jnp.logical_or, [edge == b for b in bounds])

    @pl.when(reset)
    def _():
        state_scr[...] = jnp.zeros_like(state_scr)

    nt = (((1,), (1,)), ((), ()))
    tn = (((0,), (0,)), ((), ()))
    last_row = 0 if reverse else c - 1

    def chunk(j, carry):
        jj = (n_chunks - 1 - j) if reverse else j
        rows = pl.ds(pl.multiple_of(jj * c, c), c)
        q = q_ref[rows, :].astype(F32)
        v16 = v_ref[rows, :].astype(BF16)
        states = [state_scr[h] for h in range(n_g)]
        log2f, k = _gates(z_ref[rows, :].astype(F32), 1.0 - lb_ref[...])
        hi = log2f.astype(BF16)
        lo = (log2f - hi.astype(F32)).astype(BF16)
        e2 = jnp.dot(cum_ref[...], jnp.concatenate([hi, lo], axis=1), preferred_element_type=F32)
        e_cum = e2[:, :width] + e2[:, width:]
        e_last = e_cum[last_row:last_row + 1, :]
        refs_l = _hg_reference_rows(e_cum, HG_LEAF, reverse)
        e_leaf = e_cum - refs_l[0]
        q_parts = [q * jnp.exp2(e_leaf)]
        k_parts = [k * jnp.exp2(-e_leaf)]
        for ref in refs_l[1:]:
            x = jnp.exp2(-jnp.abs(e_cum - ref))
            q_parts.append(q * x)
            k_parts.append(k * x)
        q_in = (q * jnp.exp2(e_cum)).astype(BF16)
        k_out = (k * jnp.exp2(e_last - e_cum)).astype(BF16)
        state_decay = jnp.exp2(e_last)
        wide_level = wl_ref[...]
        lane_blocks = [slice(b, b + V7X_LANES) for b in range(0, n_levels * c, V7X_LANES)]
        masks = [[(wide_level[:, lb_] == l) for l in range(lb_.start // c, lb_.stop // c)] for lb_ in lane_blocks]
        outs = []
        for h in range(n_g):
            cols = slice(h * HEAD_DIM, (h + 1) * HEAD_DIM)
            q_stack = jnp.concatenate([p[:, cols] for p in q_parts], axis=0).astype(BF16)
            k_stack = jnp.concatenate([p[:, cols] for p in k_parts], axis=0).astype(BF16)
            r = lax.dot_general(q_stack, k_stack, nt, preferred_element_type=F32)
            pieces = []
            for lb_, lane_masks in zip(lane_blocks, masks):
                piece = jnp.zeros((c, V7X_LANES), F32)
                for l, mask in zip(range(lb_.start // c, lb_.stop // c), lane_masks):
                    piece = jnp.where(mask, r[l * c:(l + 1) * c, lb_], piece)
                pieces.append(piece)
            scores = jnp.concatenate(pieces, axis=1)
            v_h = v16[:, cols]
            o = jnp.dot(scores.astype(BF16), jnp.concatenate([v_h] * n_levels, axis=0),
                        preferred_element_type=F32)
            o = o + lax.dot_general(q_in[:, cols], states[h].astype(BF16), nt, preferred_element_type=F32)
            upd = lax.dot_general(v_h, k_out[:, cols], tn, preferred_element_type=F32)
            states[h] = states[h] * state_decay[:, cols] + upd
            outs.append(o)
        for h in range(n_g):
            state_scr[h] = states[h]
        if reverse:
            g = g_ref[rows, :].astype(F32)
            fwd = fwd_ref[rows, :]
            for h in range(n_g):
                cols = slice(h * HEAD_DIM, (h + 1) * HEAD_DIM)
                tot = fwd[:, cols] + outs[h]
                ms = jnp.mean(tot * tot, axis=-1, keepdims=True)
                y = tot * lax.rsqrt(ms + EPS) * gain_ref[...]
                o_ref[rows, cols] = (y * _silu(g[:, cols])).astype(o_ref.dtype)
        else:
            for h in range(n_g):
                o_ref[rows, h * HEAD_DIM:(h + 1) * HEAD_DIM] = outs[h]
        return carry

    lax.fori_loop(0, n_chunks, chunk, 0, unroll=HG_UNROLL)


def _hgrn_dir(proj, z_group, lb, fwd_out, out_gain, lay, n_heads, reverse, out_dtype):
    t = proj.shape[0]
    lb_rows = _tile(int(np.gcd.reduce(lay.lens)), HG_BLOCK, HG_CHUNK)
    nb = t // lb_rows
    n_g = min(n_heads, HG_HEADS_PER_STEP)
    assert n_heads % n_g == 0
    n_groups = n_heads // n_g
    mats, level, n_levels = _hg_wide_constants(HG_CHUNK, HG_LEAF, reverse)
    width = n_g * HEAD_DIM

    def col(group):
        return lambda g, s: ((nb - 1 - s) if reverse else s, group * n_groups + g)

    blk = (lb_rows, width)
    const = lambda g, s: (0, 0)
    in_specs = [pl.BlockSpec(blk, col(0)), pl.BlockSpec(blk, col(z_group)), pl.BlockSpec(blk, col(3))]
    args = [proj, proj, proj]
    if reverse:
        in_specs += [pl.BlockSpec(blk, col(4)), pl.BlockSpec(blk, col(0))]
        args += [proj, fwd_out]
    in_specs.append(pl.BlockSpec((1, width), lambda g, s: (0, g)))
    args.append(lb.reshape(1, n_heads * HEAD_DIM))
    if reverse:
        in_specs.append(pl.BlockSpec((1, HEAD_DIM), const))
        args.append(out_gain.reshape(1, HEAD_DIM))
    in_specs += [pl.BlockSpec(mats.shape, const), pl.BlockSpec(level.shape, const)]
    args += [jnp.asarray(mats, BF16), jnp.asarray(level)]
    return pl.pallas_call(
        functools.partial(_hgrn_wide_body, lay, nb, n_levels, n_g, reverse),
        out_shape=jax.ShapeDtypeStruct((t, n_heads * HEAD_DIM), out_dtype),
        grid=(n_groups, nb),
        in_specs=in_specs,
        out_specs=pl.BlockSpec(blk, col(0)),
        scratch_shapes=[pltpu.VMEM((n_g, HEAD_DIM, HEAD_DIM), F32)],
        compiler_params=_params("parallel", "arbitrary"),
        name="hgrn2_bwd" if reverse else "hgrn2_fwd",
    )(*args)


def _hgrn(proj, lbs_f, lbs_b, out_gain, lay, n_heads, out_dtype):
    o_fwd = _hgrn_dir(proj, 1, lbs_f, None, None, lay, n_heads, False, F32)
    return _hgrn_dir(proj, 2, lbs_b, o_fwd, out_gain, lay, n_heads, True, out_dtype)


def _rope_tables(n_pos):
    n_freq = HEAD_DIM // 4
    pos = jnp.arange(n_pos)
    coords = jnp.stack([pos // GRID_W, pos % GRID_W], axis=-1).astype(F32)
    inv = ROPE_THETA ** (-jnp.arange(n_freq, dtype=F32) / n_freq)
    ang = coords[:, :, None] * inv
    cos = jnp.broadcast_to(jnp.cos(ang)[:, :, None, :], (n_pos, 2, 2, n_freq))
    sin = jnp.sin(ang)
    sin = jnp.stack([-sin, sin], axis=2)
    return cos.reshape(n_pos, HEAD_DIM), sin.reshape(n_pos, HEAD_DIM)


def _qk_prep_body(n_q, n_kv, q_ref, k_ref, v_ref, cos_ref, sin_ref, qg_ref, kg_ref,
                  qo_ref, ko_ref, vo_ref):
    cos = cos_ref[...]
    sin = sin_ref[...]
    lane = lax.broadcasted_iota(jnp.int32, cos.shape, 1)
    first_half = (lane % (HEAD_DIM // 2)) < (HEAD_DIM // 4)

    def norm_rope(x, gain, scale):
        ms = jnp.mean(x * x, axis=-1, keepdims=True)
        y = x * lax.rsqrt(ms + EPS) * gain
        quarter = HEAD_DIM // 4
        partner = jnp.where(first_half, pltpu.roll(y, HEAD_DIM - quarter, 1), pltpu.roll(y, quarter, 1))
        out = y * cos + partner * sin
        return out * scale if scale is not None else out

    for h in range(n_q):
        cols = slice(h * HEAD_DIM, (h + 1) * HEAD_DIM)
        qo_ref[:, cols] = norm_rope(q_ref[:, cols].astype(F32), qg_ref[...],
                                    HEAD_DIM ** -0.5 * LOG2_E).astype(qo_ref.dtype)
    for h in range(n_kv):
        cols = slice(h * HEAD_DIM, (h + 1) * HEAD_DIM)
        ko_ref[:, cols] = norm_rope(k_ref[:, cols].astype(F32), kg_ref[...], None).astype(ko_ref.dtype)
    vo_ref[...] = v_ref[...].astype(vo_ref.dtype)


def _qk_prep(proj, q_gain, k_gain, cos, sin, lay, q_col, n_q, n_kv):
    t = proj.shape[0]
    tm = _row_tile(lay, 256)
    qw, kw = n_q * HEAD_DIM, n_kv * HEAD_DIM
    assert q_col % qw == 0 and (q_col + qw) % kw == 0

    def pos_block(m):
        row = m * tm
        return ((row - _seq_start(row, lay)) // tm, 0)

    return pl.pallas_call(
        functools.partial(_qk_prep_body, n_q, n_kv),
        out_shape=(jax.ShapeDtypeStruct((t, qw), BF16), jax.ShapeDtypeStruct((t, kw), BF16),
                   jax.ShapeDtypeStruct((t, kw), BF16)),
        grid=(t // tm,),
        in_specs=[
            pl.BlockSpec((tm, qw), lambda m: (m, q_col // qw)),
            pl.BlockSpec((tm, kw), lambda m: (m, (q_col + qw) // kw)),
            pl.BlockSpec((tm, kw), lambda m: (m, (q_col + qw) // kw + 1)),
            pl.BlockSpec((tm, HEAD_DIM), pos_block),
            pl.BlockSpec((tm, HEAD_DIM), pos_block),
            pl.BlockSpec((1, HEAD_DIM), lambda m: (0, 0)),
            pl.BlockSpec((1, HEAD_DIM), lambda m: (0, 0)),
        ],
        out_specs=(pl.BlockSpec((tm, qw), lambda m: (m, 0)), pl.BlockSpec((tm, kw), lambda m: (m, 0)),
                   pl.BlockSpec((tm, kw), lambda m: (m, 0))),
        compiler_params=_params("parallel"),
        name="attn_qk_prep",
    )(proj, proj, proj, cos, sin, q_gain.reshape(1, HEAD_DIM), k_gain.reshape(1, HEAD_DIM))


def _flash_body(qb_ref, kb_ref, flag_ref, q_ref, k_ref, v_ref, o_ref, m_scr, l_scr, acc_scr):
    i = pl.program_id(1)
    flags = flag_ref[i]
    tq = q_ref.shape[0]

    @pl.when((flags & 1) != 0)
    def _():
        m_scr[...] = jnp.full_like(m_scr, -jnp.inf)
        l_scr[...] = jnp.zeros_like(l_scr)
        acc_scr[...] = jnp.zeros_like(acc_scr)

    q = q_ref[...]
    qs = jnp.concatenate([q[:, g * HEAD_DIM:(g + 1) * HEAD_DIM] for g in range(ATT_GROUP)], axis=0)
    s = lax.dot_general(qs, k_ref[...], (((1,), (1,)), ((), ())), preferred_element_type=F32)
    m_prev = m_scr[...]
    m_new = jnp.maximum(m_prev, jnp.max(s, axis=-1, keepdims=True))
    alpha = jnp.exp2(m_prev - m_new)
    p = jnp.exp2(s - m_new)
    l_scr[...] = alpha * l_scr[...] + jnp.sum(p, axis=-1, keepdims=True)
    acc_scr[...] = alpha * acc_scr[...] + jnp.dot(p.astype(BF16), v_ref[...], preferred_element_type=F32)
    m_scr[...] = m_new

    @pl.when((flags & 2) != 0)
    def _():
        out = acc_scr[...] / l_scr[...]
        for g in range(ATT_GROUP):
            o_ref[:, g * HEAD_DIM:(g + 1) * HEAD_DIM] = out[g * tq:(g + 1) * tq].astype(o_ref.dtype)


def _flash_attention(q_rot, k_rot, v16, lay, n_kv, out_dtype):
    t = q_rot.shape[0]
    g = int(np.gcd.reduce(lay.lens))
    tq = _tile(g, 256, 16)
    tk = _tile(g, FLASH_KV_ROWS, V7X_LANES)
    qb, kb, flags = [], [], []
    for st, ln in zip(lay.starts, lay.lens):
        for qi in range(ln // tq):
            for ki in range(ln // tk):
                qb.append(st // tq + qi)
                kb.append(st // tk + ki)
                flags.append((1 if ki == 0 else 0) | (2 if ki == ln // tk - 1 else 0))
    n_items = len(qb)
    gw = ATT_GROUP * HEAD_DIM
    grid_spec = pltpu.PrefetchScalarGridSpec(
        num_scalar_prefetch=3,
        grid=(n_kv, n_items),
        in_specs=[
            pl.BlockSpec((tq, gw), lambda h, i, qb, kb, fl: (qb[i], h)),
            pl.BlockSpec((tk, HEAD_DIM), lambda h, i, qb, kb, fl: (kb[i], h)),
            pl.BlockSpec((tk, HEAD_DIM), lambda h, i, qb, kb, fl: (kb[i], h)),
        ],
        out_specs=pl.BlockSpec((tq, gw), lambda h, i, qb, kb, fl: (qb[i], h)),
        scratch_shapes=[pltpu.VMEM((ATT_GROUP * tq, 1), F32), pltpu.VMEM((ATT_GROUP * tq, 1), F32),
                        pltpu.VMEM((ATT_GROUP * tq, HEAD_DIM), F32)],
    )
    return pl.pallas_call(
        _flash_body,
        out_shape=jax.ShapeDtypeStruct((t, n_kv * gw), out_dtype),
        grid_spec=grid_spec,
        compiler_params=_params("parallel", "arbitrary"),
        name="attn_flash",
    )(jnp.asarray(qb, jnp.int32), jnp.asarray(kb, jnp.int32), jnp.asarray(flags, jnp.int32),
      q_rot, k_rot, v16)


def _out_proj_body(a_ref, b_ref, wa_ref, wb_ref, x_ref, g_ref, o_ref):
    acc = jnp.dot(a_ref[...], wa_ref[...], preferred_element_type=F32)
    acc = acc + jnp.dot(b_ref[...], wb_ref[...], preferred_element_type=F32)
    o_ref[...] = x_ref[...] + g_ref[0] * acc


def _out_proj(o_hg, o_att, w_out, x, mod3, lay, layer):
    t, d = x.shape
    ka, kb = o_hg.shape[1], o_att.shape[1]
    tm = _row_tile(lay, 512)
    return pl.pallas_call(
        _out_proj_body,
        out_shape=jax.ShapeDtypeStruct((t, d), F32),
        grid=(t // tm,),
        in_specs=[
            pl.BlockSpec((tm, ka), lambda m: (m, 0)),
            pl.BlockSpec((tm, kb), lambda m: (m, 0)),
            pl.BlockSpec((ka, d), lambda m: (0, 0)),
            pl.BlockSpec((kb, d), lambda m: (1, 0)),
            pl.BlockSpec((tm, d), lambda m: (m, 0)),
            _mod_spec(lay, tm, layer, 2, d),
        ],
        out_specs=pl.BlockSpec((tm, d), lambda m: (m, 0)),
        compiler_params=_params("parallel"),
        name="mixer_out_proj",
    )(o_hg, o_att, w_out, w_out, x, mod3)


def _router_body(x_ref, sh_ref, sc_ref, gain_ref, wr_ref, o_ref, idx_ref, gate_ref):
    h = _norm_modulate(x_ref[...], gain_ref[...], sc_ref[0], sh_ref[0])
    o_ref[...] = h.astype(o_ref.dtype)
    logits = jnp.dot(h, wr_ref[...], preferred_element_type=F32, precision=lax.Precision.HIGHEST)
    n_e = logits.shape[1]
    lane = lax.broadcasted_iota(jnp.int32, logits.shape, 1)
    m1 = jnp.max(logits, axis=-1, keepdims=True)
    i1 = jnp.min(jnp.where(logits == m1, lane, n_e), axis=-1, keepdims=True)
    rest = jnp.where(lane == i1, -jnp.inf, logits)
    m2 = jnp.max(rest, axis=-1, keepdims=True)
    i2 = jnp.min(jnp.where(rest == m2, lane, n_e), axis=-1, keepdims=True)
    e = jnp.exp(m2 - m1)
    tot = 1.0 + e
    two = lax.broadcasted_iota(jnp.int32, idx_ref.shape, 1)
    idx_ref[...] = jnp.where(two == 0, i1, i2)
    gate_ref[...] = jnp.where(two == 0, 1.0 / tot, e / tot)


def _router(x, mod3, gain, lay, layer, w_router):
    t, d = x.shape
    tm = _row_tile(lay, 256)
    in_specs = [
        pl.BlockSpec((tm, d), lambda m: (m, 0)),
        _mod_spec(lay, tm, layer, 3, d),
        _mod_spec(lay, tm, layer, 4, d),
        pl.BlockSpec((1, d), lambda m: (0, 0)),
    ]
    row_spec = pl.BlockSpec((tm, d), lambda m: (m, 0))
    n_e = w_router.shape[1]
    small = pl.BlockSpec((tm, TOP_K), lambda m: (m, 0))
    return pl.pallas_call(
        _router_body,
        out_shape=(jax.ShapeDtypeStruct((t, d), F32), jax.ShapeDtypeStruct((t, TOP_K), jnp.int32),
                   jax.ShapeDtypeStruct((t, TOP_K), F32)),
        grid=(t // tm,),
        in_specs=in_specs + [pl.BlockSpec((d, n_e), lambda m: (0, 0))],
        out_specs=(row_spec, small, small),
        compiler_params=_params("parallel"), name="moe_router",
    )(x, mod3, mod3, gain.reshape(1, d), w_router)


def _swiglu_step(x16, wg_ref, wu_ref, wd_ref, acc_scr):
    gate = jnp.dot(x16, wg_ref[0], preferred_element_type=F32)
    up = jnp.dot(x16, wu_ref[0], preferred_element_type=F32)
    act = (_silu(gate) * up).astype(BF16)
    acc_scr[...] += jnp.dot(act, wd_ref[0], preferred_element_type=F32)


def _dense_ffn_body(x_ref, sh_ref, sc_ref, gain_ref, wg_ref, wu_ref, wd_ref, g_ref, o_ref, x16_scr, acc_scr):
    f = pl.program_id(1)

    @pl.when(f == 0)
    def _():
        x16_scr[...] = _norm_modulate(x_ref[...], gain_ref[...], sc_ref[0], sh_ref[0]).astype(BF16)
        acc_scr[...] = jnp.zeros_like(acc_scr)

    _swiglu_step(x16_scr[...], wg_ref, wu_ref, wd_ref, acc_scr)

    @pl.when(f == pl.num_programs(1) - 1)
    def _():
        o_ref[...] = x_ref[...] + g_ref[0] * acc_scr[...]


def _dense_ffn(x, mod3, gain, w_gate, w_up, w_down, j, lay, layer):
    t, d = x.shape
    fdim = w_gate.shape[2]
    tm = _row_tile(lay, 512)
    tf = _tile(fdim, 512, V7X_LANES)
    return pl.pallas_call(
        _dense_ffn_body,
        out_shape=jax.ShapeDtypeStruct((t, d), F32),
        grid=(t // tm, fdim // tf),
        in_specs=[
            pl.BlockSpec((tm, d), lambda m, f: (m, 0)),
            _mod_spec(lay, tm, layer, 3, d),
            _mod_spec(lay, tm, layer, 4, d),
            pl.BlockSpec((1, d), lambda m, f: (0, 0)),
            pl.BlockSpec((1, d, tf), lambda m, f: (j, 0, f)),
            pl.BlockSpec((1, d, tf), lambda m, f: (j, 0, f)),
            pl.BlockSpec((1, tf, d), lambda m, f: (j, f, 0)),
            _mod_spec(lay, tm, layer, 5, d),
        ],
        out_specs=pl.BlockSpec((tm, d), lambda m, f: (m, 0)),
        scratch_shapes=[pltpu.VMEM((tm, d), BF16), pltpu.VMEM((tm, d), F32)],
        compiler_params=_params("parallel", "arbitrary"),
        name="swiglu_residual",
    )(x, mod3, mod3, gain.reshape(1, d), w_gate, w_up, w_down, mod3)


def _expert_ffn_body(n_f, be_ref, nu_ref, tok_ref, h_hbm, wg_ref, wu_ref, wd_ref, o_ref,
                     xbuf, x16_scr, acc_scr, sems):
    m = pl.program_id(0)
    f = pl.program_id(1)
    tm = x16_scr.shape[0]
    n_used = nu_ref[0]
    used = m < n_used
    last = f == n_f - 1

    def row_copy(src_row, slot, dst_row):
        return pltpu.make_async_copy(h_hbm.at[pl.ds(src_row, 1)], xbuf.at[slot, pl.ds(dst_row, 1)],
                                     sems.at[slot])

    slot = m % 2
    share = tm // n_f
    extra = tm - share * n_f

    @pl.when(jnp.logical_and(used, f == 0))
    def _():
        @pl.when(m == 0)
        def _():
            def issue(r, carry):
                row_copy(tok_ref[r], 0, r).start()
                return carry
            lax.fori_loop(0, tm, issue, 0)

        for _ in range(tm):
            row_copy(0, slot, 0).wait()
        x16_scr[...] = xbuf[slot].astype(BF16)
        acc_scr[...] = jnp.zeros_like(acc_scr)

    first = (m + 1) * tm

    @pl.when(used)
    def _():
        for r in range(share):
            row = f * share + r
            row_copy(tok_ref[first + row], 1 - slot, row).start()
        _swiglu_step(x16_scr[...], wg_ref, wu_ref, wd_ref, acc_scr)

    @pl.when(jnp.logical_and(used, last))
    def _():
        for r in range(share * n_f, share * n_f + extra):
            row_copy(tok_ref[first + r], 1 - slot, r).start()
        o_ref[...] = acc_scr[...]

        @pl.when(m + 1 == n_used)
        def _():
            for _ in range(tm):
                row_copy(0, 1 - slot, 0).wait()

    @pl.when(jnp.logical_and(jnp.logical_not(used), last))
    def _():
        o_ref[...] = jnp.zeros_like(o_ref)


def _expert_ffn(h32, row_token, w_gate, w_up, w_down, block_expert, n_used, tm):
    d = h32.shape[1]
    r = row_token.shape[0]
    fdim = w_gate.shape[2]
    tf = _tile(fdim, 512, V7X_LANES)
    nf = fdim // tf

    def fcol(m, f, nu):
        return jnp.where(m < nu[0], f, nf - 1)

    grid_spec = pltpu.PrefetchScalarGridSpec(
        num_scalar_prefetch=3,
        grid=(r // tm, nf),
        in_specs=[
            pl.BlockSpec(memory_space=pl.ANY),
            pl.BlockSpec((1, d, tf), lambda m, f, be, nu, tok: (be[m], 0, fcol(m, f, nu))),
            pl.BlockSpec((1, d, tf), lambda m, f, be, nu, tok: (be[m], 0, fcol(m, f, nu))),
            pl.BlockSpec((1, tf, d), lambda m, f, be, nu, tok: (be[m], fcol(m, f, nu), 0)),
        ],
        out_specs=pl.BlockSpec((tm, d), lambda m, f, be, nu, tok: (m, 0)),
        scratch_shapes=[pltpu.VMEM((2, tm, d), F32), pltpu.VMEM((tm, d), BF16), pltpu.VMEM((tm, d), F32),
                        pltpu.SemaphoreType.DMA((2,))],
    )
    return pl.pallas_call(
        functools.partial(_expert_ffn_body, nf),
        out_shape=jax.ShapeDtypeStruct((r, d), F32),
        grid_spec=grid_spec,
        compiler_params=_params("arbitrary", "arbitrary"),
        name="swiglu_experts",
    )(block_expert, n_used, row_token, h32, w_gate, w_up, w_down)


def _combine_body(dest_ref, yb_hbm, x_ref, gate_ref, g_ref, o_ref, buf, sems):
    tm = x_ref.shape[0]
    i = pl.program_id(0)
    slot = i % 2

    def row_copy(src_row, slot_, k, dst_row):
        return pltpu.make_async_copy(yb_hbm.at[pl.ds(src_row, 1)], buf.at[slot_, k, pl.ds(dst_row, 1)],
                                     sems.at[slot_])

    def gather(tile, slot_):
        def issue(r, carry):
            for k in range(TOP_K):
                row_copy(dest_ref[(tile * tm + r) * TOP_K + k], slot_, k, r).start()
            return carry
        lax.fori_loop(0, tm, issue, 0)

    @pl.when(i == 0)
    def _():
        gather(0, 0)

    @pl.when(i + 1 < pl.num_programs(0))
    def _():
        gather(i + 1, 1 - slot)

    for _ in range(tm * TOP_K):
        row_copy(0, slot, 0, 0).wait()
    gates = gate_ref[...]
    y = buf[slot, 0] * gates[:, 0:1]
    for k in range(1, TOP_K):
        y = y + buf[slot, k] * gates[:, k:k + 1]
    o_ref[...] = x_ref[...] + g_ref[0] * y


def _combine(yb, dest, gates, x, mod3, lay, layer):
    t, d = x.shape
    tm = _row_tile(lay, 256)
    grid_spec = pltpu.PrefetchScalarGridSpec(
        num_scalar_prefetch=1,
        grid=(t // tm,),
        in_specs=[
            pl.BlockSpec(memory_space=pl.ANY),
            pl.BlockSpec((tm, d), lambda m, dest: (m, 0)),
            pl.BlockSpec((tm, TOP_K), lambda m, dest: (m, 0)),
            _mod_spec(lay, tm, layer, 5, d),
        ],
        out_specs=pl.BlockSpec((tm, d), lambda m, dest: (m, 0)),
        scratch_shapes=[pltpu.VMEM((2, TOP_K, tm, d), yb.dtype), pltpu.SemaphoreType.DMA((2,))],
    )
    return pl.pallas_call(
        _combine_body,
        out_shape=jax.ShapeDtypeStruct((t, d), F32),
        grid_spec=grid_spec,
        compiler_params=_params("arbitrary"),
        name="moe_combine",
    )(dest, yb, x, gates, mod3)


def _moe(x, h32, idx, gates, w_gate, w_up, w_down, first_expert, n_e, mod3, lay, layer):
    t, d = x.shape
    tm = _row_tile(lay, 512)
    n_slots = t * TOP_K
    cap = n_slots + n_e * tm
    e = idx.reshape(-1)
    onehot = (e[:, None] == jnp.arange(n_e, dtype=jnp.int32)[None, :]).astype(jnp.int32)
    csum = jnp.cumsum(onehot, axis=0)
    rank = jnp.take_along_axis(csum, e[:, None], axis=1)[:, 0] - 1
    counts = csum[-1]
    padded = (counts + tm - 1) // tm * tm
    pad_end = jnp.cumsum(padded)
    pad_start = pad_end - padded
    dest = (pad_start[e] + rank).astype(jnp.int32)
    row_token = jnp.zeros((cap,), jnp.int32).at[dest].set(jnp.arange(n_slots, dtype=jnp.int32) // TOP_K)
    block_first = jnp.arange(cap // tm, dtype=jnp.int32) * tm
    block_e = jnp.minimum(jnp.searchsorted(pad_end, block_first, side='right'), n_e - 1).astype(jnp.int32)
    n_used = (pad_end[-1:] // tm).astype(jnp.int32)
    block_e = jnp.where(block_first < pad_end[-1], block_e, block_e[jnp.maximum(n_used[0] - 1, 0)])
    yb = _expert_ffn(h32, row_token, w_gate, w_up, w_down, block_e + first_expert, n_used, tm)
    return _combine(yb, dest, gates, x, mod3, lay, layer)


def _final_norm_body(x_ref, gain_ref, o_ref):
    x = x_ref[...]
    ms = jnp.mean(x * x, axis=-1, keepdims=True)
    o_ref[...] = x * lax.rsqrt(ms + EPS) * gain_ref[...]


def _final_norm(x, gain, row0, n_rows, tm):
    d = x.shape[1]
    first = row0 // tm
    return pl.pallas_call(
        _final_norm_body,
        out_shape=jax.ShapeDtypeStruct((n_rows, d), F32),
        grid=(n_rows // tm,),
        in_specs=[pl.BlockSpec((tm, d), lambda m: (first + m, 0)), pl.BlockSpec((1, d), lambda m: (0, 0))],
        out_specs=pl.BlockSpec((tm, d), lambda m: (m, 0)),
        compiler_params=_params("parallel"),
        name="final_norm",
    )(x, gain.reshape(1, d))


def kernel(x_prompt, x_sample, c_prompt, c_sample, w_mod, b_mod, norm_mix, norm_ffn, w_in, hg_lb_logits,
           hg_out_norm, qk_norm, w_out, ffn_gate, ffn_up, ffn_down, router, exp_gate, exp_up, exp_down,
           final_norm):
    depth, d = norm_mix.shape
    lay = _layout(x_prompt, x_sample)
    n_seq = len(lay.starts)
    assert n_seq <= MOD_ROWS
    hg_heads = hg_lb_logits.shape[2] // HEAD_DIM
    kv_heads = (w_in.shape[2] - 5 * hg_heads * HEAD_DIM) // HEAD_DIM // (ATT_GROUP + 2)
    att_heads = kv_heads * ATT_GROUP
    hg_width = hg_heads * HEAD_DIM
    n_e = exp_gate.shape[1]

    x = jnp.concatenate([x_prompt.reshape(-1, d), x_sample.reshape(-1, d)], axis=0)
    c_rows = jnp.concatenate([c_prompt, c_sample, jnp.zeros((MOD_ROWS - n_seq, d), F32)], axis=0)
    mod3 = _modulation(c_rows, w_mod, b_mod).reshape(depth * MOD_ROWS, 1, 6 * d)
    lbs = _lower_bounds(hg_lb_logits)
    cos, sin = _rope_tables(max(lay.lens))

    w_in16 = w_in.astype(BF16)
    w_out16 = w_out.astype(BF16)
    dense16 = [w.astype(BF16) for w in (ffn_gate, ffn_up, ffn_down)]
    expert16 = [w.astype(BF16).reshape((-1,) + w.shape[2:]) for w in (exp_gate, exp_up, exp_down)]

    for layer in range(depth):
        proj = _in_proj(x, mod3, norm_mix[layer], w_in16[layer], lay, layer, PROJ_DTYPE)
        o_hg = _hgrn(proj, lbs[0, layer], lbs[1, layer], hg_out_norm[layer], lay, hg_heads, BF16)
        q_rot, k_rot, v16 = _qk_prep(proj, qk_norm[layer, 0], qk_norm[layer, 1], cos, sin, lay,
                                     5 * hg_width, att_heads, kv_heads)
        o_att = _flash_attention(q_rot, k_rot, v16, lay, kv_heads, BF16)
        x = _out_proj(o_hg, o_att, w_out16[layer], x, mod3, lay, layer)
        j = layer // 2
        if layer % 2 == 0:
            x = _dense_ffn(x, mod3, norm_ffn[layer], *dense16, j, lay, layer)
        else:
            h32, idx, gates = _router(x, mod3, norm_ffn[layer], lay, layer, router[j])
            x = _moe(x, h32, idx, gates, *expert16, j * n_e, n_e, mod3, lay, layer)

    tm = _row_tile(lay, 512)
    n_prompt = x_prompt.shape[0] * x_prompt.shape[1]
    y_prompt = _final_norm(x, final_norm, 0, n_prompt, tm).reshape(x_prompt.shape)
    y_sample = _final_norm(x, final_norm, n_prompt, lay.total - n_prompt, tm).reshape(x_sample.shape)
    return (y_prompt, y_sample)
```

```python
import functools
from typing import NamedTuple

import numpy as np
import jax
import jax.numpy as jnp
from jax import lax
from jax.experimental import pallas as pl
from jax.experimental.pallas import tpu as pltpu

F32 = jnp.float32
BF16 = jnp.bfloat16

EPS = 1e-6
MIN_FORGET = 1e-6
GRID_W = 64
ROPE_THETA = 10000.0
HEAD_DIM = 128
ATT_GROUP = 4
TOP_K = 2
LOG2_E = 1.4426950408889634

V7X_LANES = 128
V7X_MXU_COLS = 256
V7X_VMEM_LIMIT_BYTES = 56 * 1024 * 1024

FLASH_KV_ROWS = 4096
HG_CHUNK = 64
HG_LEAF = 8
HG_BLOCK = 512
HG_HEADS_PER_STEP = 8
HG_UNROLL = 8
FFN_COLS = 1024
MOD_ROWS = 8
PROJ_DTYPE = BF16


class Layout(NamedTuple):
    starts: tuple
    lens: tuple
    total: int


def _layout(x_prompt, x_sample):
    starts, lens, row = [], [], 0
    for arr in (x_prompt, x_sample):
        for _ in range(arr.shape[0]):
            starts.append(row)
            lens.append(arr.shape[1])
            row += arr.shape[1]
    return Layout(tuple(starts), tuple(lens), row)


def _seq_index(row, lay):
    s = jnp.int32(0)
    for st in lay.starts[1:]:
        s = s + (row >= st).astype(jnp.int32)
    return s


def _seq_start(row, lay):
    s = jnp.int32(0)
    for st in lay.starts[1:]:
        s = jnp.where(row >= st, jnp.int32(st), s)
    return s


def _tile(n, target, align):
    best = None
    t = align
    while t <= min(n, target):
        if n % t == 0:
            best = t
        t += align
    assert best is not None, (n, target, align)
    return best


def _row_tile(lay, target):
    return _tile(int(np.gcd.reduce(lay.lens)), target, 16)


def _params(*sem):
    return pltpu.CompilerParams(dimension_semantics=sem, vmem_limit_bytes=V7X_VMEM_LIMIT_BYTES)


def _silu(x):
    return x * (1.0 / (1.0 + jnp.exp(-x)))


def _lower_bounds_body(depth, l_ref, o_ref):
    for d in range(2):
        rows = [l_ref[d * depth + i:d * depth + i + 1, :] for i in range(depth)]
        m = functools.reduce(jnp.maximum, rows)
        e = [jnp.exp(r - m) for r in rows]
        tot = functools.reduce(lambda a, b: a + b, e)
        s = [ei / tot for ei in e]
        run = None
        for i in range(depth):
            run = s[i] if run is None else run + s[i]
            o_ref[d * depth + i:d * depth + i + 1, :] = run - s[0]


def _lower_bounds(lb_logits):
    two, depth, width = lb_logits.shape
    out = pl.pallas_call(
        functools.partial(_lower_bounds_body, depth),
        out_shape=jax.ShapeDtypeStruct((two * depth, width), F32),
        name="hgrn_lower_bounds",
    )(lb_logits.reshape(two * depth, width).astype(F32))
    return out.reshape(two, depth, width)


def _modulation_body(c_ref, w_ref, b_ref, o_ref):
    sc = _silu(c_ref[...]).astype(BF16)
    acc = jnp.dot(sc, w_ref[0].astype(BF16), preferred_element_type=F32)
    o_ref[0] = acc + b_ref[0]


def _modulation(c_rows, w_mod, b_mod):
    depth, d, n = w_mod.shape
    rows = c_rows.shape[0]
    tn = _tile(n, 1024, V7X_LANES)
    return pl.pallas_call(
        _modulation_body,
        out_shape=jax.ShapeDtypeStruct((depth, rows, n), F32),
        grid=(depth, n // tn),
        in_specs=[
            pl.BlockSpec((rows, d), lambda l, j: (0, 0)),
            pl.BlockSpec((1, d, tn), lambda l, j: (l, 0, j)),
            pl.BlockSpec((1, 1, tn), lambda l, j: (l, 0, j)),
        ],
        out_specs=pl.BlockSpec((1, rows, tn), lambda l, j: (l, 0, j)),
        compiler_params=_params("parallel", "parallel"),
        name="adaln_modulation",
    )(c_rows, w_mod, b_mod.reshape(depth, 1, n))


def _mod_spec(lay, tm, layer, part, d, m_axis=0):
    def index(*ids):
        return (layer * MOD_ROWS + _seq_index(ids[m_axis] * tm, lay), 0, part)
    return pl.BlockSpec((1, 1, d), index)


def _norm_modulate(x, gain, scale, shift):
    ms = jnp.mean(x * x, axis=-1, keepdims=True)
    y = x * lax.rsqrt(ms + EPS) * gain
    return y * (1.0 + scale) + shift


def _in_proj_body(x_ref, sh_ref, sc_ref, gain_ref, w_ref, o_ref):
    h = _norm_modulate(x_ref[...], gain_ref[...], sc_ref[0], sh_ref[0])
    o_ref[...] = jnp.dot(h.astype(BF16), w_ref[...], preferred_element_type=F32).astype(o_ref.dtype)


def _in_proj(x, mod3, gain, w, lay, layer, out_dtype):
    t, d = x.shape
    n = w.shape[1]
    tm = _row_tile(lay, 512)
    tn = _tile(n, n // 2 if n % (2 * V7X_MXU_COLS) == 0 else n, V7X_LANES)
    return pl.pallas_call(
        _in_proj_body,
        out_shape=jax.ShapeDtypeStruct((t, n), out_dtype),
        grid=(n // tn, t // tm),
        in_specs=[
            pl.BlockSpec((tm, d), lambda j, m: (m, 0)),
            _mod_spec(lay, tm, layer, 0, d, m_axis=1),
            _mod_spec(lay, tm, layer, 1, d, m_axis=1),
            pl.BlockSpec((1, d), lambda j, m: (0, 0)),
            pl.BlockSpec((d, tn), lambda j, m: (0, j), pipeline_mode=pl.Buffered(1)),
        ],
        out_specs=pl.BlockSpec((tm, tn), lambda j, m: (m, j)),
        compiler_params=_params("parallel", "parallel"),
        name="mixer_in_proj",
    )(x, mod3, mod3, gain.reshape(1, d), w)


def _gates(z, one_minus_lb):
    t = jnp.exp2(jnp.abs(z) * (-LOG2_E))
    r = 1.0 / (1.0 + t)
    k = one_minus_lb * jnp.where(z >= 0, t * r, r)
    log2f = jnp.log2(jnp.maximum(1.0 - k, MIN_FORGET))
    return log2f, k


def _hg_levels(c, leaf):
    widths = []
    w = leaf
    while w < c:
        widths.append(w)
        w *= 2
    return widths


def _hg_wide_constants(c, leaf, reverse):
    t = np.arange(c)[:, None]
    s = np.arange(c)[None, :]
    if reverse:
        t, s = c - 1 - t, c - 1 - s
    cum = (s <= t).astype(np.float32)
    level = np.full((c, c), -1, np.int32)
    level[(s <= t) & (t // leaf == s // leaf)] = 0
    for i, w in enumerate(_hg_levels(c, leaf)):
        level[(s <= t) & (t // (2 * w) == s // (2 * w)) & (t // w != s // w)] = i + 1
    n_levels = 1 + len(_hg_levels(c, leaf))
    wide = np.full((c, n_levels * c), -1, np.int32)
    for l in range(n_levels):
        wide[:, l * c:(l + 1) * c] = np.where(level == l, l, -1)
    return cum, wide, n_levels


def _hg_reference_rows(e_cum, leaf, reverse):
    c, width = e_cum.shape
    mid = leaf // 2 if reverse else leaf // 2 - 1
    refs = [jnp.concatenate(
        [jnp.broadcast_to(e_cum[b + mid:b + mid + 1, :], (leaf, width)) for b in range(0, c, leaf)], axis=0)]
    for w in _hg_levels(c, leaf):
        row = w if reverse else w - 1
        refs.append(jnp.concatenate(
            [jnp.broadcast_to(e_cum[p + row:p + row + 1, :], (2 * w, width)) for p in range(0, c, 2 * w)],
            axis=0))
    return refs


def _hgrn_wide_body(lay, nb, n_levels, n_g, reverse, *refs):
    if reverse:
        q_ref, z_ref, v_ref, g_ref, fwd_ref, lb_ref, gain_ref, cum_ref, wl_ref, o_ref, state_scr = refs
    else:
        q_ref, z_ref, v_ref, lb_ref, cum_ref, wl_ref, o_ref, state_scr = refs
    step = pl.program_id(1)
    c = HG_CHUNK
    n_chunks = q_ref.shape[0] // c
    width = n_g * HEAD_DIM
    blk = (nb - 1 - step) if reverse else step
    row0 = blk * (n_chunks * c)
    bounds = [st + ln for st, ln in zip(lay.starts, lay.lens)] if reverse else list(lay.starts)
    edge = row0 + n_chunks * c if reverse else row0
    reset = functools.reduce(jnp.logical_or, [edge == b for b in bounds])

    @pl.when(reset)
    def _():
        state_scr[...] = jnp.zeros_like(state_scr)

    nt = (((1,), (1,)), ((), ()))
    tn = (((0,), (0,)), ((), ()))
    last_row = 0 if reverse else c - 1

    def chunk(j, carry):
        jj = (n_chunks - 1 - j) if reverse else j
        rows = pl.ds(pl.multiple_of(jj * c, c), c)
        q = q_ref[rows, :].astype(F32)
        v16 = v_ref[rows, :].astype(BF16)
        states = [state_scr[h] for h in range(n_g)]
        log2f, k = _gates(z_ref[rows, :].astype(F32), 1.0 - lb_ref[...])
        hi = log2f.astype(BF16)
        lo = (log2f - hi.astype(F32)).astype(BF16)
        e2 = jnp.dot(cum_ref[...], jnp.concatenate([hi, lo], axis=1), preferred_element_type=F32)
        e_cum = e2[:, :width] + e2[:, width:]
        e_last = e_cum[last_row:last_row + 1, :]
        refs_l = _hg_reference_rows(e_cum, HG_LEAF, reverse)
        e_leaf = e_cum - refs_l[0]
        q_parts = [q * jnp.exp2(e_leaf)]
        k_parts = [k * jnp.exp2(-e_leaf)]
        for ref in refs_l[1:]:
            x = jnp.exp2(-jnp.abs(e_cum - ref))
            q_parts.append(q * x)
            k_parts.append(k * x)
        q_in = (q * jnp.exp2(e_cum)).astype(BF16)
        k_out = (k * jnp.exp2(e_last - e_cum)).astype(BF16)
        state_decay = jnp.exp2(e_last)
        wide_level = wl_ref[...]
        lane_blocks = [slice(b, b + V7X_LANES) for b in range(0, n_levels * c, V7X_LANES)]
        masks = [[(wide_level[:, lb_] == l) for l in range(lb_.start // c, lb_.stop // c)] for lb_ in lane_blocks]
        outs = []
        for h in range(n_g):
            cols = slice(h * HEAD_DIM, (h + 1) * HEAD_DIM)
            q_stack = jnp.concatenate([p[:, cols] for p in q_parts], axis=0).astype(BF16)
            k_stack = jnp.concatenate([p[:, cols] for p in k_parts], axis=0).astype(BF16)
            r = lax.dot_general(q_stack, k_stack, nt, preferred_element_type=F32)
            pieces = []
            for lb_, lane_masks in zip(lane_blocks, masks):
                piece = jnp.zeros((c, V7X_LANES), F32)
                for l, mask in zip(range(lb_.start // c, lb_.stop // c), lane_masks):
                    piece = jnp.where(mask, r[l * c:(l + 1) * c, lb_], piece)
                pieces.append(piece)
            scores = jnp.concatenate(pieces, axis=1)
            v_h = v16[:, cols]
            o = jnp.dot(scores.astype(BF16), jnp.concatenate([v_h] * n_levels, axis=0),
                        preferred_element_type=F32)
            o = o + lax.dot_general(q_in[:, cols], states[h].astype(BF16), nt, preferred_element_type=F32)
            upd = lax.dot_general(v_h, k_out[:, cols], tn, preferred_element_type=F32)
            states[h] = states[h] * state_decay[:, cols] + upd
            outs.append(o)
        for h in range(n_g):
            state_scr[h] = states[h]
        if reverse:
            g = g_ref[rows, :].astype(F32)
            fwd = fwd_ref[rows, :]
            for h in range(n_g):
                cols = slice(h * HEAD_DIM, (h + 1) * HEAD_DIM)
                tot = fwd[:, cols] + outs[h]
                ms = jnp.mean(tot * tot, axis=-1, keepdims=True)
                y = tot * lax.rsqrt(ms + EPS) * gain_ref[...]
                o_ref[rows, cols] = (y * _silu(g[:, cols])).astype(o_ref.dtype)
        else:
            for h in range(n_g):
                o_ref[rows, h * HEAD_DIM:(h + 1) * HEAD_DIM] = outs[h]
        return carry

    lax.fori_loop(0, n_chunks, chunk, 0, unroll=HG_UNROLL)


def _hgrn_dir(proj, z_group, lb, fwd_out, out_gain, lay, n_heads, reverse, out_dtype):
    t = proj.shape[0]
    lb_rows = _tile(int(np.gcd.reduce(lay.lens)), HG_BLOCK, HG_CHUNK)
    nb = t // lb_rows
    n_g = min(n_heads, HG_HEADS_PER_STEP)
    assert n_heads % n_g == 0
    n_groups = n_heads // n_g
    mats, level, n_levels = _hg_wide_constants(HG_CHUNK, HG_LEAF, reverse)
    width = n_g * HEAD_DIM

    def col(group):
        return lambda g, s: ((nb - 1 - s) if reverse else s, group * n_groups + g)

    blk = (lb_rows, width)
    const = lambda g, s: (0, 0)
    in_specs = [pl.BlockSpec(blk, col(0)), pl.BlockSpec(blk, col(z_group)), pl.BlockSpec(blk, col(3))]
    args = [proj, proj, proj]
    if reverse:
        in_specs += [pl.BlockSpec(blk, col(4)), pl.BlockSpec(blk, col(0))]
        args += [proj, fwd_out]
    in_specs.append(pl.BlockSpec((1, width), lambda g, s: (0, g)))
    args.append(lb.reshape(1, n_heads * HEAD_DIM))
    if reverse:
        in_specs.append(pl.BlockSpec((1, HEAD_DIM), const))
        args.append(out_gain.reshape(1, HEAD_DIM))
    in_specs += [pl.BlockSpec(mats.shape, const), pl.BlockSpec(level.shape, const)]
    args += [jnp.asarray(mats, BF16), jnp.asarray(level)]
    return pl.pallas_call(
        functools.partial(_hgrn_wide_body, lay, nb, n_levels, n_g, reverse),
        out_shape=jax.ShapeDtypeStruct((t, n_heads * HEAD_DIM), out_dtype),
        grid=(n_groups, nb),
        in_specs=in_specs,
        out_specs=pl.BlockSpec(blk, col(0)),
        scratch_shapes=[pltpu.VMEM((n_g, HEAD_DIM, HEAD_DIM), F32)],
        compiler_params=_params("parallel", "arbitrary"),
        name="hgrn2_bwd" if reverse else "hgrn2_fwd",
    )(*args)


def _hgrn(proj, lbs_f, lbs_b, out_gain, lay, n_heads, out_dtype):
    o_fwd = _hgrn_dir(proj, 1, lbs_f, None, None, lay, n_heads, False, F32)
    return _hgrn_dir(proj, 2, lbs_b, o_fwd, out_gain, lay, n_heads, True, out_dtype)


def _rope_tables(n_pos):
    n_freq = HEAD_DIM // 4
    pos = jnp.arange(n_pos)
    coords = jnp.stack([pos // GRID_W, pos % GRID_W], axis=-1).astype(F32)
    inv = ROPE_THETA ** (-jnp.arange(n_freq, dtype=F32) / n_freq)
    ang = coords[:, :, None] * inv
    cos = jnp.broadcast_to(jnp.cos(ang)[:, :, None, :], (n_pos, 2, 2, n_freq))
    sin = jnp.sin(ang)
    sin = jnp.stack([-sin, sin], axis=2)
    return cos.reshape(n_pos, HEAD_DIM), sin.reshape(n_pos, HEAD_DIM)


def _qk_prep_body(n_q, n_kv, q_ref, k_ref, v_ref, cos_ref, sin_ref, qg_ref, kg_ref, swap_ref,
                  qo_ref, ko_ref, vo_ref):
    cos = cos_ref[...]
    sin = sin_ref[...]
    swap = swap_ref[...]

    def norm_rope(x, gain, scale):
        ms = jnp.mean(x * x, axis=-1, keepdims=True)
        y = x * lax.rsqrt(ms + EPS) * gain
        partner = jnp.dot(y.astype(BF16), swap, preferred_element_type=F32)
        out = y * cos + partner * sin
        return out * scale if scale is not None else out

    for h in range(n_q):
        cols = slice(h * HEAD_DIM, (h + 1) * HEAD_DIM)
        qo_ref[:, cols] = norm_rope(q_ref[:, cols].astype(F32), qg_ref[...],
                                    HEAD_DIM ** -0.5 * LOG2_E).astype(qo_ref.dtype)
    for h in range(n_kv):
        cols = slice(h * HEAD_DIM, (h + 1) * HEAD_DIM)
        ko_ref[:, cols] = norm_rope(k_ref[:, cols].astype(F32), kg_ref[...], None).astype(ko_ref.dtype)
    vo_ref[...] = v_ref[...].astype(vo_ref.dtype)


def _qk_prep(proj, q_gain, k_gain, cos, sin, lay, q_col, n_q, n_kv):
    t = proj.shape[0]
    tm = _row_tile(lay, 256)
    qw, kw = n_q * HEAD_DIM, n_kv * HEAD_DIM
    assert q_col % qw == 0 and (q_col + qw) % kw == 0

    def pos_block(m):
        row = m * tm
        return ((row - _seq_start(row, lay)) // tm, 0)

    quarter = HEAD_DIM // 4
    lane = np.arange(HEAD_DIM)
    source = np.where(lane % (2 * quarter) < quarter, lane + quarter, lane - quarter)
    swap = (lane[:, None] == source[None, :]).astype(np.float32)

    return pl.pallas_call(
        functools.partial(_qk_prep_body, n_q, n_kv),
        out_shape=(jax.ShapeDtypeStruct((t, qw), BF16), jax.ShapeDtypeStruct((t, kw), BF16),
                   jax.ShapeDtypeStruct((t, kw), BF16)),
        grid=(t // tm,),
        in_specs=[
            pl.BlockSpec((tm, qw), lambda m: (m, q_col // qw)),
            pl.BlockSpec((tm, kw), lambda m: (m, (q_col + qw) // kw)),
            pl.BlockSpec((tm, kw), lambda m: (m, (q_col + qw) // kw + 1)),
            pl.BlockSpec((tm, HEAD_DIM), pos_block),
            pl.BlockSpec((tm, HEAD_DIM), pos_block),
            pl.BlockSpec((1, HEAD_DIM), lambda m: (0, 0)),
            pl.BlockSpec((1, HEAD_DIM), lambda m: (0, 0)),
            pl.BlockSpec((HEAD_DIM, HEAD_DIM), lambda m: (0, 0)),
        ],
        out_specs=(pl.BlockSpec((tm, qw), lambda m: (m, 0)), pl.BlockSpec((tm, kw), lambda m: (m, 0)),
                   pl.BlockSpec((tm, kw), lambda m: (m, 0))),
        compiler_params=_params("parallel"),
        name="attn_qk_prep",
    )(proj, proj, proj, cos, sin, q_gain.reshape(1, HEAD_DIM), k_gain.reshape(1, HEAD_DIM),
      jnp.asarray(swap, BF16))


def _flash_body(qb_ref, kb_ref, flag_ref, q_ref, k_ref, v_ref, o_ref, m_scr, l_scr, acc_scr):
    i = pl.program_id(1)
    flags = flag_ref[i]
    tq = q_ref.shape[0]

    @pl.when((flags & 1) != 0)
    def _():
        m_scr[...] = jnp.full_like(m_scr, -jnp.inf)
        l_scr[...] = jnp.zeros_like(l_scr)
        acc_scr[...] = jnp.zeros_like(acc_scr)

    q = q_ref[...]
    qs = jnp.concatenate([q[:, g * HEAD_DIM:(g + 1) * HEAD_DIM] for g in range(ATT_GROUP)], axis=0)
    s = lax.dot_general(qs, k_ref[...], (((1,), (1,)), ((), ())), preferred_element_type=F32)
    m_prev = m_scr[...]
    m_new = jnp.maximum(m_prev, jnp.max(s, axis=-1, keepdims=True))
    alpha = jnp.exp2(m_prev - m_new)
    p = jnp.exp2(s - m_new)
    l_scr[...] = alpha * l_scr[...] + jnp.sum(p, axis=-1, keepdims=True)
    acc_scr[...] = alpha * acc_scr[...] + jnp.dot(p.astype(BF16), v_ref[...], preferred_element_type=F32)
    m_scr[...] = m_new

    @pl.when((flags & 2) != 0)
    def _():
        out = acc_scr[...] / l_scr[...]
        for g in range(ATT_GROUP):
            o_ref[:, g * HEAD_DIM:(g + 1) * HEAD_DIM] = out[g * tq:(g + 1) * tq].astype(o_ref.dtype)


def _flash_attention(q_rot, k_rot, v16, lay, n_kv, out_dtype):
    t = q_rot.shape[0]
    g = int(np.gcd.reduce(lay.lens))
    tq = _tile(g, 256, 16)
    tk = _tile(g, FLASH_KV_ROWS, V7X_LANES)
    qb, kb, flags = [], [], []
    for st, ln in zip(lay.starts, lay.lens):
        for qi in range(ln // tq):
            for ki in range(ln // tk):
                qb.append(st // tq + qi)
                kb.append(st // tk + ki)
                flags.append((1 if ki == 0 else 0) | (2 if ki == ln // tk - 1 else 0))
    n_items = len(qb)
    gw = ATT_GROUP * HEAD_DIM
    grid_spec = pltpu.PrefetchScalarGridSpec(
        num_scalar_prefetch=3,
        grid=(n_kv, n_items),
        in_specs=[
            pl.BlockSpec((tq, gw), lambda h, i, qb, kb, fl: (qb[i], h)),
            pl.BlockSpec((tk, HEAD_DIM), lambda h, i, qb, kb, fl: (kb[i], h)),
            pl.BlockSpec((tk, HEAD_DIM), lambda h, i, qb, kb, fl: (kb[i], h)),
        ],
        out_specs=pl.BlockSpec((tq, gw), lambda h, i, qb, kb, fl: (qb[i], h)),
        scratch_shapes=[pltpu.VMEM((ATT_GROUP * tq, 1), F32), pltpu.VMEM((ATT_GROUP * tq, 1), F32),
                        pltpu.VMEM((ATT_GROUP * tq, HEAD_DIM), F32)],
    )
    return pl.pallas_call(
        _flash_body,
        out_shape=jax.ShapeDtypeStruct((t, n_kv * gw), out_dtype),
        grid_spec=grid_spec,
        compiler_params=_params("parallel", "arbitrary"),
        name="attn_flash",
    )(jnp.asarray(qb, jnp.int32), jnp.asarray(kb, jnp.int32), jnp.asarray(flags, jnp.int32),
      q_rot, k_rot, v16)


def _out_proj_body(a_ref, b_ref, wa_ref, wb_ref, x_ref, g_ref, o_ref):
    acc = jnp.dot(a_ref[...], wa_ref[...], preferred_element_type=F32)
    acc = acc + jnp.dot(b_ref[...], wb_ref[...], preferred_element_type=F32)
    o_ref[...] = x_ref[...] + g_ref[0] * acc


def _out_proj(o_hg, o_att, w_out, x, mod3, lay, layer):
    t, d = x.shape
    ka, kb = o_hg.shape[1], o_att.shape[1]
    tm = _row_tile(lay, 512)
    return pl.pallas_call(
        _out_proj_body,
        out_shape=jax.ShapeDtypeStruct((t, d), F32),
        grid=(t // tm,),
        in_specs=[
            pl.BlockSpec((tm, ka), lambda m: (m, 0)),
            pl.BlockSpec((tm, kb), lambda m: (m, 0)),
            pl.BlockSpec((ka, d), lambda m: (0, 0)),
            pl.BlockSpec((kb, d), lambda m: (1, 0)),
            pl.BlockSpec((tm, d), lambda m: (m, 0)),
            _mod_spec(lay, tm, layer, 2, d),
        ],
        out_specs=pl.BlockSpec((tm, d), lambda m: (m, 0)),
        compiler_params=_params("parallel"),
        name="mixer_out_proj",
    )(o_hg, o_att, w_out, w_out, x, mod3)


def _router_body(x_ref, sh_ref, sc_ref, gain_ref, wr_ref, o_ref, idx_ref, gate_ref):
    h = _norm_modulate(x_ref[...], gain_ref[...], sc_ref[0], sh_ref[0])
    o_ref[...] = h.astype(o_ref.dtype)
    logits = jnp.dot(h, wr_ref[...], preferred_element_type=F32, precision=lax.Precision.HIGHEST)
    n_e = logits.shape[1]
    lane = lax.broadcasted_iota(jnp.int32, logits.shape, 1)
    m1 = jnp.max(logits, axis=-1, keepdims=True)
    i1 = jnp.min(jnp.where(logits == m1, lane, n_e), axis=-1, keepdims=True)
    rest = jnp.where(lane == i1, -jnp.inf, logits)
    m2 = jnp.max(rest, axis=-1, keepdims=True)
    i2 = jnp.min(jnp.where(rest == m2, lane, n_e), axis=-1, keepdims=True)
    e = jnp.exp(m2 - m1)
    tot = 1.0 + e
    two = lax.broadcasted_iota(jnp.int32, idx_ref.shape, 1)
    idx_ref[...] = jnp.where(two == 0, i1, i2)
    gate_ref[...] = jnp.where(two == 0, 1.0 / tot, e / tot)


def _router(x, mod3, gain, lay, layer, w_router):
    t, d = x.shape
    tm = _row_tile(lay, 256)
    in_specs = [
        pl.BlockSpec((tm, d), lambda m: (m, 0)),
        _mod_spec(lay, tm, layer, 3, d),
        _mod_spec(lay, tm, layer, 4, d),
        pl.BlockSpec((1, d), lambda m: (0, 0)),
    ]
    row_spec = pl.BlockSpec((tm, d), lambda m: (m, 0))
    n_e = w_router.shape[1]
    small = pl.BlockSpec((tm, TOP_K), lambda m: (m, 0))
    return pl.pallas_call(
        _router_body,
        out_shape=(jax.ShapeDtypeStruct((t, d), F32), jax.ShapeDtypeStruct((t, TOP_K), jnp.int32),
                   jax.ShapeDtypeStruct((t, TOP_K), F32)),
        grid=(t // tm,),
        in_specs=in_specs + [pl.BlockSpec((d, n_e), lambda m: (0, 0))],
        out_specs=(row_spec, small, small),
        compiler_params=_params("parallel"), name="moe_router",
    )(x, mod3, mod3, gain.reshape(1, d), w_router)


def _swiglu_step(x16, wg_ref, wu_ref, wd_ref, acc_scr):
    gate = jnp.dot(x16, wg_ref[0], preferred_element_type=F32)
    up = jnp.dot(x16, wu_ref[0], preferred_element_type=F32)
    act = (_silu(gate) * up).astype(BF16)
    acc_scr[...] += jnp.dot(act, wd_ref[0], preferred_element_type=F32)


def _dense_ffn_body(x_ref, sh_ref, sc_ref, gain_ref, wg_ref, wu_ref, wd_ref, g_ref, o_ref, x16_scr, acc_scr):
    f = pl.program_id(1)

    @pl.when(f == 0)
    def _():
        x16_scr[...] = _norm_modulate(x_ref[...], gain_ref[...], sc_ref[0], sh_ref[0]).astype(BF16)
        acc_scr[...] = jnp.zeros_like(acc_scr)

    _swiglu_step(x16_scr[...], wg_ref, wu_ref, wd_ref, acc_scr)

    @pl.when(f == pl.num_programs(1) - 1)
    def _():
        o_ref[...] = x_ref[...] + g_ref[0] * acc_scr[...]


def _dense_ffn(x, mod3, gain, w_gate, w_up, w_down, j, lay, layer):
    t, d = x.shape
    fdim = w_gate.shape[2]
    tm = _row_tile(lay, 512)
    tf = _tile(fdim, FFN_COLS, V7X_LANES)
    return pl.pallas_call(
        _dense_ffn_body,
        out_shape=jax.ShapeDtypeStruct((t, d), F32),
        grid=(t // tm, fdim // tf),
        in_specs=[
            pl.BlockSpec((tm, d), lambda m, f: (m, 0)),
            _mod_spec(lay, tm, layer, 3, d),
            _mod_spec(lay, tm, layer, 4, d),
            pl.BlockSpec((1, d), lambda m, f: (0, 0)),
            pl.BlockSpec((1, d, tf), lambda m, f: (j, 0, f)),
            pl.BlockSpec((1, d, tf), lambda m, f: (j, 0, f)),
            pl.BlockSpec((1, tf, d), lambda m, f: (j, f, 0)),
            _mod_spec(lay, tm, layer, 5, d),
        ],
        out_specs=pl.BlockSpec((tm, d), lambda m, f: (m, 0)),
        scratch_shapes=[pltpu.VMEM((tm, d), BF16), pltpu.VMEM((tm, d), F32)],
        compiler_params=_params("parallel", "arbitrary"),
        name="swiglu_residual",
    )(x, mod3, mod3, gain.reshape(1, d), w_gate, w_up, w_down, mod3)


def _expert_ffn_body(n_f, be_ref, nu_ref, tok_ref, h_hbm, wg_ref, wu_ref, wd_ref, o_ref,
                     xbuf, x16_scr, acc_scr, sems):
    m = pl.program_id(0)
    f = pl.program_id(1)
    tm = x16_scr.shape[0]
    n_used = nu_ref[0]
    used = m < n_used
    last = f == n_f - 1

    def row_copy(src_row, slot, dst_row):
        return pltpu.make_async_copy(h_hbm.at[pl.ds(src_row, 1)], xbuf.at[slot, pl.ds(dst_row, 1)],
                                     sems.at[slot])

    slot = m % 2
    share = tm // n_f
    extra = tm - share * n_f

    @pl.when(jnp.logical_and(used, f == 0))
    def _():
        @pl.when(m == 0)
        def _():
            def issue(r, carry):
                row_copy(tok_ref[r], 0, r).start()
                return carry
            lax.fori_loop(0, tm, issue, 0)

        for _ in range(tm):
            row_copy(0, slot, 0).wait()
        x16_scr[...] = xbuf[slot].astype(BF16)
        acc_scr[...] = jnp.zeros_like(acc_scr)

    first = (m + 1) * tm

    @pl.when(used)
    def _():
        for r in range(share):
            row = f * share + r
            row_copy(tok_ref[first + row], 1 - slot, row).start()
        _swiglu_step(x16_scr[...], wg_ref, wu_ref, wd_ref, acc_scr)

    @pl.when(jnp.logical_and(used, last))
    def _():
        for r in range(share * n_f, share * n_f + extra):
            row_copy(tok_ref[first + r], 1 - slot, r).start()
        o_ref[...] = acc_scr[...]

        @pl.when(m + 1 == n_used)
        def _():
            for _ in range(tm):
                row_copy(0, 1 - slot, 0).wait()

    @pl.when(jnp.logical_and(jnp.logical_not(used), last))
    def _():
        o_ref[...] = jnp.zeros_like(o_ref)


def _expert_ffn(h32, row_token, w_gate, w_up, w_down, block_expert, n_used, tm):
    d = h32.shape[1]
    r = row_token.shape[0]
    fdim = w_gate.shape[2]
    tf = _tile(fdim, FFN_COLS, V7X_LANES)
    nf = fdim // tf

    def fcol(m, f, nu):
        return jnp.where(m < nu[0], f, nf - 1)

    grid_spec = pltpu.PrefetchScalarGridSpec(
        num_scalar_prefetch=3,
        grid=(r // tm, nf),
        in_specs=[
            pl.BlockSpec(memory_space=pl.ANY),
            pl.BlockSpec((1, d, tf), lambda m, f, be, nu, tok: (be[m], 0, fcol(m, f, nu))),
            pl.BlockSpec((1, d, tf), lambda m, f, be, nu, tok: (be[m], 0, fcol(m, f, nu))),
            pl.BlockSpec((1, tf, d), lambda m, f, be, nu, tok: (be[m], fcol(m, f, nu), 0)),
        ],
        out_specs=pl.BlockSpec((tm, d), lambda m, f, be, nu, tok: (m, 0)),
        scratch_shapes=[pltpu.VMEM((2, tm, d), F32), pltpu.VMEM((tm, d), BF16), pltpu.VMEM((tm, d), F32),
                        pltpu.SemaphoreType.DMA((2,))],
    )
    return pl.pallas_call(
        functools.partial(_expert_ffn_body, nf),
        out_shape=jax.ShapeDtypeStruct((r, d), F32),
        grid_spec=grid_spec,
        compiler_params=_params("arbitrary", "arbitrary"),
        name="swiglu_experts",
    )(block_expert, n_used, row_token, h32, w_gate, w_up, w_down)


def _combine_body(dest_ref, yb_hbm, x_ref, gate_ref, g_ref, o_ref, buf, sems):
    tm = x_ref.shape[0]
    i = pl.program_id(0)
    slot = i % 2

    def row_copy(src_row, slot_, k, dst_row):
        return pltpu.make_async_copy(yb_hbm.at[pl.ds(src_row, 1)], buf.at[slot_, k, pl.ds(dst_row, 1)],
                                     sems.at[slot_])

    def gather(tile, slot_):
        def issue(r, carry):
            for k in range(TOP_K):
                row_copy(dest_ref[(tile * tm + r) * TOP_K + k], slot_, k, r).start()
            return carry
        lax.fori_loop(0, tm, issue, 0)

    @pl.when(i == 0)
    def _():
        gather(0, 0)

    @pl.when(i + 1 < pl.num_programs(0))
    def _():
        gather(i + 1, 1 - slot)

    for _ in range(tm * TOP_K):
        row_copy(0, slot, 0, 0).wait()
    gates = gate_ref[...]
    y = buf[slot, 0] * gates[:, 0:1]
    for k in range(1, TOP_K):
        y = y + buf[slot, k] * gates[:, k:k + 1]
    o_ref[...] = x_ref[...] + g_ref[0] * y


def _combine(yb, dest, gates, x, mod3, lay, layer):
    t, d = x.shape
    tm = _row_tile(lay, 256)
    grid_spec = pltpu.PrefetchScalarGridSpec(
        num_scalar_prefetch=1,
        grid=(t // tm,),
        in_specs=[
            pl.BlockSpec(memory_space=pl.ANY),
            pl.BlockSpec((tm, d), lambda m, dest: (m, 0)),
            pl.BlockSpec((tm, TOP_K), lambda m, dest: (m, 0)),
            _mod_spec(lay, tm, layer, 5, d),
        ],
        out_specs=pl.BlockSpec((tm, d), lambda m, dest: (m, 0)),
        scratch_shapes=[pltpu.VMEM((2, TOP_K, tm, d), yb.dtype), pltpu.SemaphoreType.DMA((2,))],
    )
    return pl.pallas_call(
        _combine_body,
        out_shape=jax.ShapeDtypeStruct((t, d), F32),
        grid_spec=grid_spec,
        compiler_params=_params("arbitrary"),
        name="moe_combine",
    )(dest, yb, x, gates, mod3)


def _moe(x, h32, idx, gates, w_gate, w_up, w_down, first_expert, n_e, mod3, lay, layer):
    t, d = x.shape
    tm = _row_tile(lay, 512)
    n_slots = t * TOP_K
    cap = n_slots + n_e * tm
    e = idx.reshape(-1)
    onehot = (e[:, None] == jnp.arange(n_e, dtype=jnp.int32)[None, :]).astype(jnp.int32)
    csum = jnp.cumsum(onehot, axis=0)
    rank = jnp.take_along_axis(csum, e[:, None], axis=1)[:, 0] - 1
    counts = csum[-1]
    padded = (counts + tm - 1) // tm * tm
    pad_end = jnp.cumsum(padded)
    pad_start = pad_end - padded
    dest = (pad_start[e] + rank).astype(jnp.int32)
    row_token = jnp.zeros((cap,), jnp.int32).at[dest].set(jnp.arange(n_slots, dtype=jnp.int32) // TOP_K)
    block_first = jnp.arange(cap // tm, dtype=jnp.int32) * tm
    block_e = jnp.minimum(jnp.searchsorted(pad_end, block_first, side='right'), n_e - 1).astype(jnp.int32)
    n_used = (pad_end[-1:] // tm).astype(jnp.int32)
    block_e = jnp.where(block_first < pad_end[-1], block_e, block_e[jnp.maximum(n_used[0] - 1, 0)])
    yb = _expert_ffn(h32, row_token, w_gate, w_up, w_down, block_e + first_expert, n_used, tm)
    return _combine(yb, dest, gates, x, mod3, lay, layer)


def _final_norm_body(x_ref, gain_ref, o_ref):
    x = x_ref[...]
    ms = jnp.mean(x * x, axis=-1, keepdims=True)
    o_ref[...] = x * lax.rsqrt(ms + EPS) * gain_ref[...]


def _final_norm(x, gain, row0, n_rows, tm):
    d = x.shape[1]
    first = row0 // tm
    return pl.pallas_call(
        _final_norm_body,
        out_shape=jax.ShapeDtypeStruct((n_rows, d), F32),
        grid=(n_rows // tm,),
        in_specs=[pl.BlockSpec((tm, d), lambda m: (first + m, 0)), pl.BlockSpec((1, d), lambda m: (0, 0))],
        out_specs=pl.BlockSpec((tm, d), lambda m: (m, 0)),
        compiler_params=_params("parallel"),
        name="final_norm",
    )(x, gain.reshape(1, d))


def kernel(x_prompt, x_sample, c_prompt, c_sample, w_mod, b_mod, norm_mix, norm_ffn, w_in, hg_lb_logits,
           hg_out_norm, qk_norm, w_out, ffn_gate, ffn_up, ffn_down, router, exp_gate, exp_up, exp_down,
           final_norm):
    depth, d = norm_mix.shape
    lay = _layout(x_prompt, x_sample)
    n_seq = len(lay.starts)
    assert n_seq <= MOD_ROWS
    hg_heads = hg_lb_logits.shape[2] // HEAD_DIM
    kv_heads = (w_in.shape[2] - 5 * hg_heads * HEAD_DIM) // HEAD_DIM // (ATT_GROUP + 2)
    att_heads = kv_heads * ATT_GROUP
    hg_width = hg_heads * HEAD_DIM
    n_e = exp_gate.shape[1]

    x = jnp.concatenate([x_prompt.reshape(-1, d), x_sample.reshape(-1, d)], axis=0)
    c_rows = jnp.concatenate([c_prompt, c_sample, jnp.zeros((MOD_ROWS - n_seq, d), F32)], axis=0)
    mod3 = _modulation(c_rows, w_mod, b_mod).reshape(depth * MOD_ROWS, 1, 6 * d)
    lbs = _lower_bounds(hg_lb_logits)
    cos, sin = _rope_tables(max(lay.lens))

    w_in16 = w_in.astype(BF16)
    w_out16 = w_out.astype(BF16)
    dense16 = [w.astype(BF16) for w in (ffn_gate, ffn_up, ffn_down)]
    expert16 = [w.astype(BF16).reshape((-1,) + w.shape[2:]) for w in (exp_gate, exp_up, exp_down)]

    for layer in range(depth):
        proj = _in_proj(x, mod3, norm_mix[layer], w_in16[layer], lay, layer, PROJ_DTYPE)
        o_hg = _hgrn(proj, lbs[0, layer], lbs[1, layer], hg_out_norm[layer], lay, hg_heads, BF16)
        q_rot, k_rot, v16 = _qk_prep(proj, qk_norm[layer, 0], qk_norm[layer, 1], cos, sin, lay,
                                     5 * hg_width, att_heads, kv_heads)
        o_att = _flash_attention(q_rot, k_rot, v16, lay, kv_heads, BF16)
        x = _out_proj(o_hg, o_att, w_out16[layer], x, mod3, lay, layer)
        j = layer // 2
        if layer % 2 == 0:
            x = _dense_ffn(x, mod3, norm_ffn[layer], *dense16, j, lay, layer)
        else:
            h32, idx, gates = _router(x, mod3, norm_ffn[layer], lay, layer, router[j])
            x = _moe(x, h32, idx, gates, *expert16, j * n_e, n_e, mod3, lay, layer)

    tm = _row_tile(lay, 512)
    n_prompt = x_prompt.shape[0] * x_prompt.shape[1]
    y_prompt = _final_norm(x, final_norm, 0, n_prompt, tm).reshape(x_prompt.shape)
    y_sample = _final_norm(x, final_norm, n_prompt, lay.total - n_prompt, tm).reshape(x_sample.shape)
    return (y_prompt, y_sample)
```

```python
import functools
from typing import NamedTuple

import numpy as np
import jax
import jax.numpy as jnp
from jax import lax
from jax.experimental import pallas as pl
from jax.experimental.pallas import tpu as pltpu

F32 = jnp.float32
BF16 = jnp.bfloat16

EPS = 1e-6
MIN_FORGET = 1e-6
GRID_W = 64
ROPE_THETA = 10000.0
HEAD_DIM = 128
ATT_GROUP = 4
TOP_K = 2
LOG2_E = 1.4426950408889634

V7X_LANES = 128
V7X_MXU_COLS = 256
V7X_VMEM_LIMIT_BYTES = 56 * 1024 * 1024

FLASH_KV_ROWS = 4096
HG_CHUNK = 64
HG_LEAF = 8
HG_BLOCK = 512
HG_HEADS_PER_STEP = 8
HG_UNROLL = 8
FFN_COLS = 1024
MOD_ROWS = 8
PROJ_DTYPE = BF16


class Layout(NamedTuple):
    starts: tuple
    lens: tuple
    total: int


def _layout(x_prompt, x_sample):
    starts, lens, row = [], [], 0
    for arr in (x_prompt, x_sample):
        for _ in range(arr.shape[0]):
            starts.append(row)
            lens.append(arr.shape[1])
            row += arr.shape[1]
    return Layout(tuple(starts), tuple(lens), row)


def _seq_index(row, lay):
    s = jnp.int32(0)
    for st in lay.starts[1:]:
        s = s + (row >= st).astype(jnp.int32)
    return s


def _seq_start(row, lay):
    s = jnp.int32(0)
    for st in lay.starts[1:]:
        s = jnp.where(row >= st, jnp.int32(st), s)
    return s


def _tile(n, target, align):
    best = None
    t = align
    while t <= min(n, target):
        if n % t == 0:
            best = t
        t += align
    assert best is not None, (n, target, align)
    return best


def _row_tile(lay, target):
    return _tile(int(np.gcd.reduce(lay.lens)), target, 16)


def _params(*sem):
    return pltpu.CompilerParams(dimension_semantics=sem, vmem_limit_bytes=V7X_VMEM_LIMIT_BYTES)


def _silu(x):
    return x * (1.0 / (1.0 + jnp.exp(-x)))


def _lower_bounds_body(depth, l_ref, o_ref):
    for d in range(2):
        rows = [l_ref[d * depth + i:d * depth + i + 1, :] for i in range(depth)]
        m = functools.reduce(jnp.maximum, rows)
        e = [jnp.exp(r - m) for r in rows]
        tot = functools.reduce(lambda a, b: a + b, e)
        s = [ei / tot for ei in e]
        run = None
        for i in range(depth):
            run = s[i] if run is None else run + s[i]
            o_ref[d * depth + i:d * depth + i + 1, :] = run - s[0]


def _lower_bounds(lb_logits):
    two, depth, width = lb_logits.shape
    out = pl.pallas_call(
        functools.partial(_lower_bounds_body, depth),
        out_shape=jax.ShapeDtypeStruct((two * depth, width), F32),
        name="hgrn_lower_bounds",
    )(lb_logits.reshape(two * depth, width).astype(F32))
    return out.reshape(two, depth, width)


def _modulation_body(c_ref, w_ref, b_ref, o_ref):
    sc = _silu(c_ref[...]).astype(BF16)
    acc = jnp.dot(sc, w_ref[0].astype(BF16), preferred_element_type=F32)
    o_ref[0] = acc + b_ref[0]


def _modulation(c_rows, w_mod, b_mod):
    depth, d, n = w_mod.shape
    rows = c_rows.shape[0]
    tn = _tile(n, 1024, V7X_LANES)
    return pl.pallas_call(
        _modulation_body,
        out_shape=jax.ShapeDtypeStruct((depth, rows, n), F32),
        grid=(depth, n // tn),
        in_specs=[
            pl.BlockSpec((rows, d), lambda l, j: (0, 0)),
            pl.BlockSpec((1, d, tn), lambda l, j: (l, 0, j)),
            pl.BlockSpec((1, 1, tn), lambda l, j: (l, 0, j)),
        ],
        out_specs=pl.BlockSpec((1, rows, tn), lambda l, j: (l, 0, j)),
        compiler_params=_params("parallel", "parallel"),
        name="adaln_modulation",
    )(c_rows, w_mod, b_mod.reshape(depth, 1, n))


def _mod_spec(lay, tm, layer, part, d, m_axis=0):
    def index(*ids):
        return (layer * MOD_ROWS + _seq_index(ids[m_axis] * tm, lay), 0, part)
    return pl.BlockSpec((1, 1, d), index)


def _norm_modulate(x, gain, scale, shift):
    ms = jnp.mean(x * x, axis=-1, keepdims=True)
    y = x * lax.rsqrt(ms + EPS) * gain
    return y * (1.0 + scale) + shift


def _in_proj_body(x_ref, sh_ref, sc_ref, gain_ref, w_ref, o_ref):
    h = _norm_modulate(x_ref[...], gain_ref[...], sc_ref[0], sh_ref[0])
    o_ref[...] = jnp.dot(h.astype(BF16), w_ref[...], preferred_element_type=F32).astype(o_ref.dtype)


def _in_proj(x, mod3, gain, w, lay, layer, out_dtype):
    t, d = x.shape
    n = w.shape[1]
    tm = _row_tile(lay, 512)
    tn = _tile(n, n // 2 if n % (2 * V7X_MXU_COLS) == 0 else n, V7X_LANES)
    return pl.pallas_call(
        _in_proj_body,
        out_shape=jax.ShapeDtypeStruct((t, n), out_dtype),
        grid=(n // tn, t // tm),
        in_specs=[
            pl.BlockSpec((tm, d), lambda j, m: (m, 0)),
            _mod_spec(lay, tm, layer, 0, d, m_axis=1),
            _mod_spec(lay, tm, layer, 1, d, m_axis=1),
            pl.BlockSpec((1, d), lambda j, m: (0, 0)),
            pl.BlockSpec((d, tn), lambda j, m: (0, j), pipeline_mode=pl.Buffered(1)),
        ],
        out_specs=pl.BlockSpec((tm, tn), lambda j, m: (m, j)),
        compiler_params=_params("parallel", "parallel"),
        name="mixer_in_proj",
    )(x, mod3, mod3, gain.reshape(1, d), w)


def _gates(z, one_minus_lb):
    t = jnp.exp2(jnp.abs(z) * (-LOG2_E))
    r = 1.0 / (1.0 + t)
    k = one_minus_lb * jnp.where(z >= 0, t * r, r)
    log2f = jnp.log2(jnp.maximum(1.0 - k, MIN_FORGET))
    return log2f, k


def _hg_levels(c, leaf):
    widths = []
    w = leaf
    while w < c:
        widths.append(w)
        w *= 2
    return widths


def _hg_wide_constants(c, leaf, reverse):
    t = np.arange(c)[:, None]
    s = np.arange(c)[None, :]
    if reverse:
        t, s = c - 1 - t, c - 1 - s
    cum = (s <= t).astype(np.float32)
    level = np.full((c, c), -1, np.int32)
    level[(s <= t) & (t // leaf == s // leaf)] = 0
    for i, w in enumerate(_hg_levels(c, leaf)):
        level[(s <= t) & (t // (2 * w) == s // (2 * w)) & (t // w != s // w)] = i + 1
    n_levels = 1 + len(_hg_levels(c, leaf))
    wide = np.full((c, n_levels * c), -1, np.int32)
    for l in range(n_levels):
        wide[:, l * c:(l + 1) * c] = np.where(level == l, l, -1)
    return cum, wide, n_levels


def _hg_reference_rows(e_cum, leaf, reverse):
    c, width = e_cum.shape
    mid = leaf // 2 if reverse else leaf // 2 - 1
    refs = [jnp.concatenate(
        [jnp.broadcast_to(e_cum[b + mid:b + mid + 1, :], (leaf, width)) for b in range(0, c, leaf)], axis=0)]
    for w in _hg_levels(c, leaf):
        row = w if reverse else w - 1
        refs.append(jnp.concatenate(
            [jnp.broadcast_to(e_cum[p + row:p + row + 1, :], (2 * w, width)) for p in range(0, c, 2 * w)],
            axis=0))
    return refs


def _hgrn_wide_body(lay, nb, n_levels, n_g, reverse, *refs):
    if reverse:
        q_ref, z_ref, v_ref, g_ref, fwd_ref, lb_ref, gain_ref, cum_ref, wl_ref, o_ref, state_scr = refs
    else:
        q_ref, z_ref, v_ref, lb_ref, cum_ref, wl_ref, o_ref, state_scr = refs
    step = pl.program_id(1)
    c = HG_CHUNK
    n_chunks = q_ref.shape[0] // c
    width = n_g * HEAD_DIM
    blk = (nb - 1 - step) if reverse else step
    row0 = blk * (n_chunks * c)
    bounds = [st + ln for st, ln in zip(lay.starts, lay.lens)] if reverse else list(lay.starts)
    edge = row0 + n_chunks * c if reverse else row0
    reset = functools.reduce(jnp.logical_or, [edge == b for b in bounds])

    @pl.when(reset)
    def _():
        state_scr[...] = jnp.zeros_like(state_scr)

    nt = (((1,), (1,)), ((), ()))
    tn = (((0,), (0,)), ((), ()))
    last_row = 0 if reverse else c - 1

    def chunk(j, carry):
        jj = (n_chunks - 1 - j) if reverse else j
        rows = pl.ds(pl.multiple_of(jj * c, c), c)
        q = q_ref[rows, :].astype(F32)
        v16 = v_ref[rows, :].astype(BF16)
        states = [state_scr[h] for h in range(n_g)]
        log2f, k = _gates(z_ref[rows, :].astype(F32), 1.0 - lb_ref[...])
        hi = log2f.astype(BF16)
        lo = (log2f - hi.astype(F32)).astype(BF16)
        e2 = jnp.dot(cum_ref[...], jnp.concatenate([hi, lo], axis=1), preferred_element_type=F32)
        e_cum = e2[:, :width] + e2[:, width:]
        e_last = e_cum[last_row:last_row + 1, :]
        refs_l = _hg_reference_rows(e_cum, HG_LEAF, reverse)
        e_leaf = e_cum - refs_l[0]
        q_parts = [q * jnp.exp2(e_leaf)]
        k_parts = [k * jnp.exp2(-e_leaf)]
        for ref in refs_l[1:]:
            x = jnp.exp2(-jnp.abs(e_cum - ref))
            q_parts.append(q * x)
            k_parts.append(k * x)
        q_in = (q * jnp.exp2(e_cum)).astype(BF16)
        k_out = (k * jnp.exp2(e_last - e_cum)).astype(BF16)
        state_decay = jnp.exp2(e_last)
        wide_level = wl_ref[...]
        lane_blocks = [slice(b, b + V7X_LANES) for b in range(0, n_levels * c, V7X_LANES)]
        masks = [[(wide_level[:, lb_] == l) for l in range(lb_.start // c, lb_.stop // c)] for lb_ in lane_blocks]
        outs = []
        for h in range(n_g):
            cols = slice(h * HEAD_DIM, (h + 1) * HEAD_DIM)
            q_stack = jnp.concatenate([p[:, cols] for p in q_parts], axis=0).astype(BF16)
            k_stack = jnp.concatenate([p[:, cols] for p in k_parts], axis=0).astype(BF16)
            r = lax.dot_general(q_stack, k_stack, nt, preferred_element_type=F32)
            pieces = []
            for lb_, lane_masks in zip(lane_blocks, masks):
                piece = jnp.zeros((c, V7X_LANES), F32)
                for l, mask in zip(range(lb_.start // c, lb_.stop // c), lane_masks):
                    piece = jnp.where(mask, r[l * c:(l + 1) * c, lb_], piece)
                pieces.append(piece)
            scores = jnp.concatenate(pieces, axis=1)
            v_h = v16[:, cols]
            o = jnp.dot(scores.astype(BF16), jnp.concatenate([v_h] * n_levels, axis=0),
                        preferred_element_type=F32)
            o = o + lax.dot_general(q_in[:, cols], states[h].astype(BF16), nt, preferred_element_type=F32)
            upd = lax.dot_general(v_h, k_out[:, cols], tn, preferred_element_type=F32)
            states[h] = states[h] * state_decay[:, cols] + upd
            outs.append(o)
        for h in range(n_g):
            state_scr[h] = states[h]
        if reverse:
            g = g_ref[rows, :].astype(F32)
            fwd = fwd_ref[rows, :]
            for h in range(n_g):
                cols = slice(h * HEAD_DIM, (h + 1) * HEAD_DIM)
                tot = fwd[:, cols] + outs[h]
                ms = jnp.mean(tot * tot, axis=-1, keepdims=True)
                y = tot * lax.rsqrt(ms + EPS) * gain_ref[...]
                o_ref[rows, cols] = (y * _silu(g[:, cols])).astype(o_ref.dtype)
        else:
            for h in range(n_g):
                o_ref[rows, h * HEAD_DIM:(h + 1) * HEAD_DIM] = outs[h]
        return carry

    lax.fori_loop(0, n_chunks, chunk, 0, unroll=HG_UNROLL)


def _hgrn_dir(proj, z_group, lb, fwd_out, out_gain, lay, n_heads, reverse, out_dtype):
    t = proj.shape[0]
    lb_rows = _tile(int(np.gcd.reduce(lay.lens)), HG_BLOCK, HG_CHUNK)
    nb = t // lb_rows
    n_g = min(n_heads, HG_HEADS_PER_STEP)
    assert n_heads % n_g == 0
    n_groups = n_heads // n_g
    mats, level, n_levels = _hg_wide_constants(HG_CHUNK, HG_LEAF, reverse)
    width = n_g * HEAD_DIM

    def col(group):
        return lambda g, s: ((nb - 1 - s) if reverse else s, group * n_groups + g)

    blk = (lb_rows, width)
    const = lambda g, s: (0, 0)
    in_specs = [pl.BlockSpec(blk, col(0)), pl.BlockSpec(blk, col(z_group)), pl.BlockSpec(blk, col(3))]
    args = [proj, proj, proj]
    if reverse:
        in_specs += [pl.BlockSpec(blk, col(4)), pl.BlockSpec(blk, col(0))]
        args += [proj, fwd_out]
    in_specs.append(pl.BlockSpec((1, width), lambda g, s: (0, g)))
    args.append(lb.reshape(1, n_heads * HEAD_DIM))
    if reverse:
        in_specs.append(pl.BlockSpec((1, HEAD_DIM), const))
        args.append(out_gain.reshape(1, HEAD_DIM))
    in_specs += [pl.BlockSpec(mats.shape, const), pl.BlockSpec(level.shape, const)]
    args += [jnp.asarray(mats, BF16), jnp.asarray(level)]
    return pl.pallas_call(
        functools.partial(_hgrn_wide_body, lay, nb, n_levels, n_g, reverse),
        out_shape=jax.ShapeDtypeStruct((t, n_heads * HEAD_DIM), out_dtype),
        grid=(n_groups, nb),
        in_specs=in_specs,
        out_specs=pl.BlockSpec(blk, col(0)),
        scratch_shapes=[pltpu.VMEM((n_g, HEAD_DIM, HEAD_DIM), F32)],
        compiler_params=_params("parallel", "arbitrary"),
        name="hgrn2_bwd" if reverse else "hgrn2_fwd",
    )(*args)


def _hgrn(proj, lbs_f, lbs_b, out_gain, lay, n_heads, out_dtype):
    o_fwd = _hgrn_dir(proj, 1, lbs_f, None, None, lay, n_heads, False, F32)
    return _hgrn_dir(proj, 2, lbs_b, o_fwd, out_gain, lay, n_heads, True, out_dtype)


def _rope_tables(n_pos):
    n_freq = HEAD_DIM // 4
    pos = jnp.arange(n_pos)
    coords = jnp.stack([pos // GRID_W, pos % GRID_W], axis=-1).astype(F32)
    inv = ROPE_THETA ** (-jnp.arange(n_freq, dtype=F32) / n_freq)
    ang = coords[:, :, None] * inv
    cos = jnp.broadcast_to(jnp.cos(ang)[:, :, None, :], (n_pos, 2, 2, n_freq))
    sin = jnp.sin(ang)
    sin = jnp.stack([-sin, sin], axis=2)
    return cos.reshape(n_pos, HEAD_DIM), sin.reshape(n_pos, HEAD_DIM)


def _qk_prep_body(n_q, n_kv, q_ref, k_ref, v_ref, cos_ref, sin_ref, qg_ref, kg_ref, swap_ref,
                  qo_ref, ko_ref, vo_ref):
    cos = cos_ref[...]
    sin = sin_ref[...]
    swap = swap_ref[...]

    def norm_rope(x, gain, scale):
        ms = jnp.mean(x * x, axis=-1, keepdims=True)
        y = x * lax.rsqrt(ms + EPS) * gain
        partner = jnp.dot(y.astype(BF16), swap, preferred_element_type=F32)
        out = y * cos + partner * sin
        return out * scale if scale is not None else out

    for h in range(n_q):
        cols = slice(h * HEAD_DIM, (h + 1) * HEAD_DIM)
        qo_ref[:, cols] = norm_rope(q_ref[:, cols].astype(F32), qg_ref[...],
                                    HEAD_DIM ** -0.5 * LOG2_E).astype(qo_ref.dtype)
    for h in range(n_kv):
        cols = slice(h * HEAD_DIM, (h + 1) * HEAD_DIM)
        ko_ref[:, cols] = norm_rope(k_ref[:, cols].astype(F32), kg_ref[...], None).astype(ko_ref.dtype)
    vo_ref[...] = v_ref[...].astype(vo_ref.dtype)


def _qk_prep(proj, q_gain, k_gain, cos, sin, lay, q_col, n_q, n_kv):
    t = proj.shape[0]
    tm = _row_tile(lay, 256)
    qw, kw = n_q * HEAD_DIM, n_kv * HEAD_DIM
    assert q_col % qw == 0 and (q_col + qw) % kw == 0

    def pos_block(m):
        row = m * tm
        return ((row - _seq_start(row, lay)) // tm, 0)

    quarter = HEAD_DIM // 4
    lane = np.arange(HEAD_DIM)
    source = np.where(lane % (2 * quarter) < quarter, lane + quarter, lane - quarter)
    swap = (lane[:, None] == source[None, :]).astype(np.float32)

    return pl.pallas_call(
        functools.partial(_qk_prep_body, n_q, n_kv),
        out_shape=(jax.ShapeDtypeStruct((t, qw), BF16), jax.ShapeDtypeStruct((t, kw), BF16),
                   jax.ShapeDtypeStruct((t, kw), BF16)),
        grid=(t // tm,),
        in_specs=[
            pl.BlockSpec((tm, qw), lambda m: (m, q_col // qw)),
            pl.BlockSpec((tm, kw), lambda m: (m, (q_col + qw) // kw)),
            pl.BlockSpec((tm, kw), lambda m: (m, (q_col + qw) // kw + 1)),
            pl.BlockSpec((tm, HEAD_DIM), pos_block),
            pl.BlockSpec((tm, HEAD_DIM), pos_block),
            pl.BlockSpec((1, HEAD_DIM), lambda m: (0, 0)),
            pl.BlockSpec((1, HEAD_DIM), lambda m: (0, 0)),
            pl.BlockSpec((HEAD_DIM, HEAD_DIM), lambda m: (0, 0)),
        ],
        out_specs=(pl.BlockSpec((tm, qw), lambda m: (m, 0)), pl.BlockSpec((tm, kw), lambda m: (m, 0)),
                   pl.BlockSpec((tm, kw), lambda m: (m, 0))),
        compiler_params=_params("parallel"),
        name="attn_qk_prep",
    )(proj, proj, proj, cos, sin, q_gain.reshape(1, HEAD_DIM), k_gain.reshape(1, HEAD_DIM),
      jnp.asarray(swap, BF16))


def _flash_body(with_cast, qb_ref, kb_ref, flag_ref, q_ref, k_ref, v_ref, *rest):
    if with_cast:
        w32_ref, o_ref, w16_ref, m_scr, l_scr, acc_scr = rest
        w16_ref[...] = w32_ref[...].astype(BF16)
    else:
        o_ref, m_scr, l_scr, acc_scr = rest
    i = pl.program_id(1)
    flags = flag_ref[i]
    tq = q_ref.shape[0]

    @pl.when((flags & 1) != 0)
    def _():
        m_scr[...] = jnp.full_like(m_scr, -jnp.inf)
        l_scr[...] = jnp.zeros_like(l_scr)
        acc_scr[...] = jnp.zeros_like(acc_scr)

    q = q_ref[...]
    qs = jnp.concatenate([q[:, g * HEAD_DIM:(g + 1) * HEAD_DIM] for g in range(ATT_GROUP)], axis=0)
    s = lax.dot_general(qs, k_ref[...], (((1,), (1,)), ((), ())), preferred_element_type=F32)
    m_prev = m_scr[...]
    m_new = jnp.maximum(m_prev, jnp.max(s, axis=-1, keepdims=True))
    alpha = jnp.exp2(m_prev - m_new)
    p = jnp.exp2(s - m_new)
    l_scr[...] = alpha * l_scr[...] + jnp.sum(p, axis=-1, keepdims=True)
    acc_scr[...] = alpha * acc_scr[...] + jnp.dot(p.astype(BF16), v_ref[...], preferred_element_type=F32)
    m_scr[...] = m_new

    @pl.when((flags & 2) != 0)
    def _():
        out = acc_scr[...] / l_scr[...]
        for g in range(ATT_GROUP):
            o_ref[:, g * HEAD_DIM:(g + 1) * HEAD_DIM] = out[g * tq:(g + 1) * tq].astype(o_ref.dtype)


def _flash_attention(q_rot, k_rot, v16, lay, n_kv, out_dtype, cast_src=None, cast_unit=0, cast_units=1):
    t = q_rot.shape[0]
    g = int(np.gcd.reduce(lay.lens))
    tq = _tile(g, 256, 16)
    tk = _tile(g, FLASH_KV_ROWS, V7X_LANES)
    qb, kb, flags = [], [], []
    for st, ln in zip(lay.starts, lay.lens):
        for qi in range(ln // tq):
            for ki in range(ln // tk):
                qb.append(st // tq + qi)
                kb.append(st // tk + ki)
                flags.append((1 if ki == 0 else 0) | (2 if ki == ln // tk - 1 else 0))
    n_items = len(qb)
    gw = ATT_GROUP * HEAD_DIM
    in_specs = [
        pl.BlockSpec((tq, gw), lambda h, i, qb, kb, fl: (qb[i], h)),
        pl.BlockSpec((tk, HEAD_DIM), lambda h, i, qb, kb, fl: (kb[i], h)),
        pl.BlockSpec((tk, HEAD_DIM), lambda h, i, qb, kb, fl: (kb[i], h)),
    ]
    out_specs = pl.BlockSpec((tq, gw), lambda h, i, qb, kb, fl: (qb[i], h))
    out_shape = jax.ShapeDtypeStruct((t, n_kv * gw), out_dtype)
    args = [q_rot, k_rot, v16]
    if cast_src is not None:
        rows, cols = cast_src.shape[0] // cast_units, cast_src.shape[1]
        n_steps = n_kv * n_items
        chunk = min(c for c in range(16, rows + 1, 16) if rows % c == 0 and rows // c <= n_steps)
        n_chunks = rows // chunk

        def chunk_of(h, i):
            return jnp.minimum(h * n_items + i, n_chunks - 1)

        in_specs.append(pl.BlockSpec((chunk, cols),
                                     lambda h, i, qb, kb, fl: (cast_unit * n_chunks + chunk_of(h, i), 0)))
        out_specs = (out_specs, pl.BlockSpec((chunk, cols), lambda h, i, qb, kb, fl: (chunk_of(h, i), 0)))
        out_shape = (out_shape, jax.ShapeDtypeStruct((rows, cols), BF16))
        args.append(cast_src)
    grid_spec = pltpu.PrefetchScalarGridSpec(
        num_scalar_prefetch=3,
        grid=(n_kv, n_items),
        in_specs=in_specs,
        out_specs=out_specs,
        scratch_shapes=[pltpu.VMEM((ATT_GROUP * tq, 1), F32), pltpu.VMEM((ATT_GROUP * tq, 1), F32),
                        pltpu.VMEM((ATT_GROUP * tq, HEAD_DIM), F32)],
    )
    return pl.pallas_call(
        functools.partial(_flash_body, cast_src is not None),
        out_shape=out_shape,
        grid_spec=grid_spec,
        compiler_params=_params("arbitrary", "arbitrary"),
        name="attn_flash",
    )(jnp.asarray(qb, jnp.int32), jnp.asarray(kb, jnp.int32), jnp.asarray(flags, jnp.int32), *args)


def _out_proj_body(a_ref, b_ref, wa_ref, wb_ref, x_ref, g_ref, o_ref):
    acc = jnp.dot(a_ref[...], wa_ref[...], preferred_element_type=F32)
    acc = acc + jnp.dot(b_ref[...], wb_ref[...], preferred_element_type=F32)
    o_ref[...] = x_ref[...] + g_ref[0] * acc


def _out_proj(o_hg, o_att, w_out, x, mod3, lay, layer):
    t, d = x.shape
    ka, kb = o_hg.shape[1], o_att.shape[1]
    tm = _row_tile(lay, 512)
    return pl.pallas_call(
        _out_proj_body,
        out_shape=jax.ShapeDtypeStruct((t, d), F32),
        grid=(t // tm,),
        in_specs=[
            pl.BlockSpec((tm, ka), lambda m: (m, 0)),
            pl.BlockSpec((tm, kb), lambda m: (m, 0)),
            pl.BlockSpec((ka, d), lambda m: (0, 0)),
            pl.BlockSpec((kb, d), lambda m: (1, 0)),
            pl.BlockSpec((tm, d), lambda m: (m, 0)),
            _mod_spec(lay, tm, layer, 2, d),
        ],
        out_specs=pl.BlockSpec((tm, d), lambda m: (m, 0)),
        compiler_params=_params("parallel"),
        name="mixer_out_proj",
    )(o_hg, o_att, w_out, w_out, x, mod3)


def _router_body(x_ref, sh_ref, sc_ref, gain_ref, wr_ref, o_ref, idx_ref, gate_ref):
    h = _norm_modulate(x_ref[...], gain_ref[...], sc_ref[0], sh_ref[0])
    o_ref[...] = h.astype(o_ref.dtype)
    logits = jnp.dot(h, wr_ref[...], preferred_element_type=F32, precision=lax.Precision.HIGHEST)
    n_e = logits.shape[1]
    lane = lax.broadcasted_iota(jnp.int32, logits.shape, 1)
    m1 = jnp.max(logits, axis=-1, keepdims=True)
    i1 = jnp.min(jnp.where(logits == m1, lane, n_e), axis=-1, keepdims=True)
    rest = jnp.where(lane == i1, -jnp.inf, logits)
    m2 = jnp.max(rest, axis=-1, keepdims=True)
    i2 = jnp.min(jnp.where(rest == m2, lane, n_e), axis=-1, keepdims=True)
    e = jnp.exp(m2 - m1)
    tot = 1.0 + e
    two = lax.broadcasted_iota(jnp.int32, idx_ref.shape, 1)
    idx_ref[...] = jnp.where(two == 0, i1, i2)
    gate_ref[...] = jnp.where(two == 0, 1.0 / tot, e / tot)


def _router(x, mod3, gain, lay, layer, w_router):
    t, d = x.shape
    tm = _row_tile(lay, 256)
    in_specs = [
        pl.BlockSpec((tm, d), lambda m: (m, 0)),
        _mod_spec(lay, tm, layer, 3, d),
        _mod_spec(lay, tm, layer, 4, d),
        pl.BlockSpec((1, d), lambda m: (0, 0)),
    ]
    row_spec = pl.BlockSpec((tm, d), lambda m: (m, 0))
    n_e = w_router.shape[1]
    small = pl.BlockSpec((tm, TOP_K), lambda m: (m, 0))
    return pl.pallas_call(
        _router_body,
        out_shape=(jax.ShapeDtypeStruct((t, d), F32), jax.ShapeDtypeStruct((t, TOP_K), jnp.int32),
                   jax.ShapeDtypeStruct((t, TOP_K), F32)),
        grid=(t // tm,),
        in_specs=in_specs + [pl.BlockSpec((d, n_e), lambda m: (0, 0))],
        out_specs=(row_spec, small, small),
        compiler_params=_params("parallel"), name="moe_router",
    )(x, mod3, mod3, gain.reshape(1, d), w_router)


def _swiglu_step(x16, wg_ref, wu_ref, wd_ref, acc_scr):
    gate = jnp.dot(x16, wg_ref[0], preferred_element_type=F32)
    up = jnp.dot(x16, wu_ref[0], preferred_element_type=F32)
    act = (_silu(gate) * up).astype(BF16)
    acc_scr[...] += jnp.dot(act, wd_ref[0], preferred_element_type=F32)


def _dense_ffn_body(x_ref, sh_ref, sc_ref, gain_ref, wg_ref, wu_ref, wd_ref, g_ref, o_ref, x16_scr, acc_scr):
    f = pl.program_id(1)

    @pl.when(f == 0)
    def _():
        x16_scr[...] = _norm_modulate(x_ref[...], gain_ref[...], sc_ref[0], sh_ref[0]).astype(BF16)
        acc_scr[...] = jnp.zeros_like(acc_scr)

    _swiglu_step(x16_scr[...], wg_ref, wu_ref, wd_ref, acc_scr)

    @pl.when(f == pl.num_programs(1) - 1)
    def _():
        o_ref[...] = x_ref[...] + g_ref[0] * acc_scr[...]


def _dense_ffn(x, mod3, gain, w_gate, w_up, w_down, j, lay, layer):
    t, d = x.shape
    fdim = w_gate.shape[2]
    tm = _row_tile(lay, 512)
    tf = _tile(fdim, FFN_COLS, V7X_LANES)
    return pl.pallas_call(
        _dense_ffn_body,
        out_shape=jax.ShapeDtypeStruct((t, d), F32),
        grid=(t // tm, fdim // tf),
        in_specs=[
            pl.BlockSpec((tm, d), lambda m, f: (m, 0)),
            _mod_spec(lay, tm, layer, 3, d),
            _mod_spec(lay, tm, layer, 4, d),
            pl.BlockSpec((1, d), lambda m, f: (0, 0)),
            pl.BlockSpec((1, d, tf), lambda m, f: (j, 0, f)),
            pl.BlockSpec((1, d, tf), lambda m, f: (j, 0, f)),
            pl.BlockSpec((1, tf, d), lambda m, f: (j, f, 0)),
            _mod_spec(lay, tm, layer, 5, d),
        ],
        out_specs=pl.BlockSpec((tm, d), lambda m, f: (m, 0)),
        scratch_shapes=[pltpu.VMEM((tm, d), BF16), pltpu.VMEM((tm, d), F32)],
        compiler_params=_params("parallel", "arbitrary"),
        name="swiglu_residual",
    )(x, mod3, mod3, gain.reshape(1, d), w_gate, w_up, w_down, mod3)


def _expert_ffn_body(n_f, be_ref, nu_ref, tok_ref, h_hbm, wg_ref, wu_ref, wd_ref, o_ref,
                     xbuf, x16_scr, acc_scr, sems):
    m = pl.program_id(0)
    f = pl.program_id(1)
    tm = x16_scr.shape[0]
    n_used = nu_ref[0]
    used = m < n_used
    last = f == n_f - 1

    def row_copy(src_row, slot, dst_row):
        return pltpu.make_async_copy(h_hbm.at[pl.ds(src_row, 1)], xbuf.at[slot, pl.ds(dst_row, 1)],
                                     sems.at[slot])

    slot = m % 2
    share = tm // n_f
    extra = tm - share * n_f

    @pl.when(jnp.logical_and(used, f == 0))
    def _():
        @pl.when(m == 0)
        def _():
            def issue(r, carry):
                row_copy(tok_ref[r], 0, r).start()
                return carry
            lax.fori_loop(0, tm, issue, 0)

        for _ in range(tm):
            row_copy(0, slot, 0).wait()
        x16_scr[...] = xbuf[slot].astype(BF16)
        acc_scr[...] = jnp.zeros_like(acc_scr)

    first = (m + 1) * tm

    @pl.when(used)
    def _():
        for r in range(share):
            row = f * share + r
            row_copy(tok_ref[first + row], 1 - slot, row).start()
        _swiglu_step(x16_scr[...], wg_ref, wu_ref, wd_ref, acc_scr)

    @pl.when(jnp.logical_and(used, last))
    def _():
        for r in range(share * n_f, share * n_f + extra):
            row_copy(tok_ref[first + r], 1 - slot, r).start()
        o_ref[...] = acc_scr[...]

        @pl.when(m + 1 == n_used)
        def _():
            for _ in range(tm):
                row_copy(0, 1 - slot, 0).wait()

    @pl.when(jnp.logical_and(jnp.logical_not(used), last))
    def _():
        o_ref[...] = jnp.zeros_like(o_ref)


def _expert_ffn(h32, row_token, w_gate, w_up, w_down, block_expert, n_used, tm, down_first=0):
    d = h32.shape[1]
    r = row_token.shape[0]
    fdim = w_gate.shape[2]
    tf = _tile(fdim, FFN_COLS, V7X_LANES)
    nf = fdim // tf

    def fcol(m, f, nu):
        return jnp.where(m < nu[0], f, nf - 1)

    grid_spec = pltpu.PrefetchScalarGridSpec(
        num_scalar_prefetch=3,
        grid=(r // tm, nf),
        in_specs=[
            pl.BlockSpec(memory_space=pl.ANY),
            pl.BlockSpec((1, d, tf), lambda m, f, be, nu, tok: (be[m], 0, fcol(m, f, nu))),
            pl.BlockSpec((1, d, tf), lambda m, f, be, nu, tok: (be[m], 0, fcol(m, f, nu))),
            pl.BlockSpec((1, tf, d), lambda m, f, be, nu, tok: (down_first + be[m], fcol(m, f, nu), 0)),
        ],
        out_specs=pl.BlockSpec((tm, d), lambda m, f, be, nu, tok: (m, 0)),
        scratch_shapes=[pltpu.VMEM((2, tm, d), F32), pltpu.VMEM((tm, d), BF16), pltpu.VMEM((tm, d), F32),
                        pltpu.SemaphoreType.DMA((2,))],
    )
    return pl.pallas_call(
        functools.partial(_expert_ffn_body, nf),
        out_shape=jax.ShapeDtypeStruct((r, d), F32),
        grid_spec=grid_spec,
        compiler_params=_params("arbitrary", "arbitrary"),
        name="swiglu_experts",
    )(block_expert, n_used, row_token, h32, w_gate, w_up, w_down)


def _combine_body(dest_ref, yb_hbm, x_ref, gate_ref, g_ref, o_ref, buf, sems):
    tm = x_ref.shape[0]
    i = pl.program_id(0)
    slot = i % 2

    def row_copy(src_row, slot_, k, dst_row):
        return pltpu.make_async_copy(yb_hbm.at[pl.ds(src_row, 1)], buf.at[slot_, k, pl.ds(dst_row, 1)],
                                     sems.at[slot_])

    def gather(tile, slot_):
        def issue(r, carry):
            for k in range(TOP_K):
                row_copy(dest_ref[(tile * tm + r) * TOP_K + k], slot_, k, r).start()
            return carry
        lax.fori_loop(0, tm, issue, 0)

    @pl.when(i == 0)
    def _():
        gather(0, 0)

    @pl.when(i + 1 < pl.num_programs(0))
    def _():
        gather(i + 1, 1 - slot)

    for _ in range(tm * TOP_K):
        row_copy(0, slot, 0, 0).wait()
    gates = gate_ref[...]
    y = buf[slot, 0] * gates[:, 0:1]
    for k in range(1, TOP_K):
        y = y + buf[slot, k] * gates[:, k:k + 1]
    o_ref[...] = x_ref[...] + g_ref[0] * y


def _combine(yb, dest, gates, x, mod3, lay, layer):
    t, d = x.shape
    tm = _row_tile(lay, 256)
    grid_spec = pltpu.PrefetchScalarGridSpec(
        num_scalar_prefetch=1,
        grid=(t // tm,),
        in_specs=[
            pl.BlockSpec(memory_space=pl.ANY),
            pl.BlockSpec((tm, d), lambda m, dest: (m, 0)),
            pl.BlockSpec((tm, TOP_K), lambda m, dest: (m, 0)),
            _mod_spec(lay, tm, layer, 5, d),
        ],
        out_specs=pl.BlockSpec((tm, d), lambda m, dest: (m, 0)),
        scratch_shapes=[pltpu.VMEM((2, TOP_K, tm, d), yb.dtype), pltpu.SemaphoreType.DMA((2,))],
    )
    return pl.pallas_call(
        _combine_body,
        out_shape=jax.ShapeDtypeStruct((t, d), F32),
        grid_spec=grid_spec,
        compiler_params=_params("arbitrary"),
        name="moe_combine",
    )(dest, yb, x, gates, mod3)


def _moe(x, h32, idx, gates, w_gate, w_up, w_down, down_first, n_e, mod3, lay, layer):
    t, d = x.shape
    tm = _row_tile(lay, 512)
    n_slots = t * TOP_K
    cap = n_slots + n_e * tm
    e = idx.reshape(-1)
    onehot = (e[:, None] == jnp.arange(n_e, dtype=jnp.int32)[None, :]).astype(jnp.int32)
    csum = jnp.cumsum(onehot, axis=0)
    rank = jnp.take_along_axis(csum, e[:, None], axis=1)[:, 0] - 1
    counts = csum[-1]
    padded = (counts + tm - 1) // tm * tm
    pad_end = jnp.cumsum(padded)
    pad_start = pad_end - padded
    dest = (pad_start[e] + rank).astype(jnp.int32)
    row_token = jnp.zeros((cap,), jnp.int32).at[dest].set(jnp.arange(n_slots, dtype=jnp.int32) // TOP_K)
    block_first = jnp.arange(cap // tm, dtype=jnp.int32) * tm
    block_e = jnp.minimum(jnp.searchsorted(pad_end, block_first, side='right'), n_e - 1).astype(jnp.int32)
    n_used = (pad_end[-1:] // tm).astype(jnp.int32)
    block_e = jnp.where(block_first < pad_end[-1], block_e, block_e[jnp.maximum(n_used[0] - 1, 0)])
    yb = _expert_ffn(h32, row_token, w_gate, w_up, w_down, block_e, n_used, tm, down_first)
    return _combine(yb, dest, gates, x, mod3, lay, layer)


def _final_norm_body(x_ref, gain_ref, o_ref):
    x = x_ref[...]
    ms = jnp.mean(x * x, axis=-1, keepdims=True)
    o_ref[...] = x * lax.rsqrt(ms + EPS) * gain_ref[...]


def _final_norm(x, gain, row0, n_rows, tm):
    d = x.shape[1]
    first = row0 // tm
    return pl.pallas_call(
        _final_norm_body,
        out_shape=jax.ShapeDtypeStruct((n_rows, d), F32),
        grid=(n_rows // tm,),
        in_specs=[pl.BlockSpec((tm, d), lambda m: (first + m, 0)), pl.BlockSpec((1, d), lambda m: (0, 0))],
        out_specs=pl.BlockSpec((tm, d), lambda m: (m, 0)),
        compiler_params=_params("parallel"),
        name="final_norm",
    )(x, gain.reshape(1, d))


def kernel(x_prompt, x_sample, c_prompt, c_sample, w_mod, b_mod, norm_mix, norm_ffn, w_in, hg_lb_logits,
           hg_out_norm, qk_norm, w_out, ffn_gate, ffn_up, ffn_down, router, exp_gate, exp_up, exp_down,
           final_norm):
    depth, d = norm_mix.shape
    lay = _layout(x_prompt, x_sample)
    n_seq = len(lay.starts)
    assert n_seq <= MOD_ROWS
    hg_heads = hg_lb_logits.shape[2] // HEAD_DIM
    kv_heads = (w_in.shape[2] - 5 * hg_heads * HEAD_DIM) // HEAD_DIM // (ATT_GROUP + 2)
    att_heads = kv_heads * ATT_GROUP
    hg_width = hg_heads * HEAD_DIM
    n_e = exp_gate.shape[1]

    x = jnp.concatenate([x_prompt.reshape(-1, d), x_sample.reshape(-1, d)], axis=0)
    c_rows = jnp.concatenate([c_prompt, c_sample, jnp.zeros((MOD_ROWS - n_seq, d), F32)], axis=0)
    mod3 = _modulation(c_rows, w_mod, b_mod).reshape(depth * MOD_ROWS, 1, 6 * d)
    lbs = _lower_bounds(hg_lb_logits)
    cos, sin = _rope_tables(max(lay.lens))

    w_in16 = w_in.astype(BF16)
    w_out16 = w_out.astype(BF16)
    dense16 = [w.astype(BF16) for w in (ffn_gate, ffn_up, ffn_down)]
    n_moe = exp_gate.shape[0]
    assert depth == 2 * n_moe, "one attention call per (MoE layer, gate / up) weight slab"
    exp_gate2 = exp_gate.reshape(-1, exp_gate.shape[-1])
    exp_up2 = exp_up.reshape(-1, exp_up.shape[-1])
    exp_down16 = exp_down.astype(BF16).reshape((-1,) + exp_down.shape[2:])
    expert16 = [[None] * n_moe, [None] * n_moe]

    for layer in range(depth):
        proj = _in_proj(x, mod3, norm_mix[layer], w_in16[layer], lay, layer, PROJ_DTYPE)
        o_hg = _hgrn(proj, lbs[0, layer], lbs[1, layer], hg_out_norm[layer], lay, hg_heads, BF16)
        q_rot, k_rot, v16 = _qk_prep(proj, qk_norm[layer, 0], qk_norm[layer, 1], cos, sin, lay,
                                     5 * hg_width, att_heads, kv_heads)
        j = layer // 2
        slab = exp_gate2 if layer % 2 == 0 else exp_up2
        o_att, slab16 = _flash_attention(q_rot, k_rot, v16, lay, kv_heads, BF16, cast_src=slab,
                                         cast_unit=j, cast_units=n_moe)
        expert16[layer % 2][j] = slab16.reshape(n_e, d, -1)
        x = _out_proj(o_hg, o_att, w_out16[layer], x, mod3, lay, layer)
        if layer % 2 == 0:
            x = _dense_ffn(x, mod3, norm_ffn[layer], *dense16, j, lay, layer)
        else:
            h32, idx, gates = _router(x, mod3, norm_ffn[layer], lay, layer, router[j])
            x = _moe(x, h32, idx, gates, expert16[0][j], expert16[1][j], exp_down16, j * n_e, n_e, mod3,
                     lay, layer)

    tm = _row_tile(lay, 512)
    n_prompt = x_prompt.shape[0] * x_prompt.shape[1]
    y_prompt = _final_norm(x, final_norm, 0, n_prompt, tm).reshape(x_prompt.shape)
    y_sample = _final_norm(x, final_norm, n_prompt, lay.total - n_prompt, tm).reshape(x_sample.shape)
    return (y_prompt, y_sample)
```

```python
import functools
from typing import NamedTuple

import numpy as np
import jax
import jax.numpy as jnp
from jax import lax
from jax.experimental import pallas as pl
from jax.experimental.pallas import tpu as pltpu

F32 = jnp.float32
BF16 = jnp.bfloat16

EPS = 1e-6
MIN_FORGET = 1e-6
GRID_W = 64
ROPE_THETA = 10000.0
HEAD_DIM = 128
ATT_GROUP = 4
TOP_K = 2
LOG2_E = 1.4426950408889634

V7X_LANES = 128
V7X_MXU_COLS = 256
V7X_VMEM_LIMIT_BYTES = 56 * 1024 * 1024

FLASH_KV_ROWS = 4096
HG_CHUNK = 64
HG_LEAF = 8
HG_BLOCK = 512
HG_HEADS_PER_STEP = 8
HG_UNROLL = 8
FFN_COLS = 1024
MOD_ROWS = 8
PROJ_DTYPE = BF16


class Layout(NamedTuple):
    starts: tuple
    lens: tuple
    total: int


def _layout(x_prompt, x_sample):
    starts, lens, row = [], [], 0
    for arr in (x_prompt, x_sample):
        for _ in range(arr.shape[0]):
            starts.append(row)
            lens.append(arr.shape[1])
            row += arr.shape[1]
    return Layout(tuple(starts), tuple(lens), row)


def _seq_index(row, lay):
    s = jnp.int32(0)
    for st in lay.starts[1:]:
        s = s + (row >= st).astype(jnp.int32)
    return s


def _seq_start(row, lay):
    s = jnp.int32(0)
    for st in lay.starts[1:]:
        s = jnp.where(row >= st, jnp.int32(st), s)
    return s


def _tile(n, target, align):
    best = None
    t = align
    while t <= min(n, target):
        if n % t == 0:
            best = t
        t += align
    assert best is not None, (n, target, align)
    return best


def _row_tile(lay, target):
    return _tile(int(np.gcd.reduce(lay.lens)), target, 16)


def _params(*sem):
    return pltpu.CompilerParams(dimension_semantics=sem, vmem_limit_bytes=V7X_VMEM_LIMIT_BYTES)


def _silu(x):
    return x * (1.0 / (1.0 + jnp.exp(-x)))


def _lower_bounds_body(depth, l_ref, o_ref):
    for d in range(2):
        rows = [l_ref[d * depth + i:d * depth + i + 1, :] for i in range(depth)]
        m = functools.reduce(jnp.maximum, rows)
        e = [jnp.exp(r - m) for r in rows]
        tot = functools.reduce(lambda a, b: a + b, e)
        s = [ei / tot for ei in e]
        run = None
        for i in range(depth):
            run = s[i] if run is None else run + s[i]
            o_ref[d * depth + i:d * depth + i + 1, :] = run - s[0]


def _lower_bounds(lb_logits):
    two, depth, width = lb_logits.shape
    out = pl.pallas_call(
        functools.partial(_lower_bounds_body, depth),
        out_shape=jax.ShapeDtypeStruct((two * depth, width), F32),
        name="hgrn_lower_bounds",
    )(lb_logits.reshape(two * depth, width).astype(F32))
    return out.reshape(two, depth, width)


def _modulation_body(c_ref, w_ref, b_ref, o_ref):
    sc = _silu(c_ref[...]).astype(BF16)
    acc = jnp.dot(sc, w_ref[0].astype(BF16), preferred_element_type=F32)
    o_ref[0] = acc + b_ref[0]


def _modulation(c_rows, w_mod, b_mod):
    depth, d, n = w_mod.shape
    rows = c_rows.shape[0]
    tn = _tile(n, 1024, V7X_LANES)
    return pl.pallas_call(
        _modulation_body,
        out_shape=jax.ShapeDtypeStruct((depth, rows, n), F32),
        grid=(depth, n // tn),
        in_specs=[
            pl.BlockSpec((rows, d), lambda l, j: (0, 0)),
            pl.BlockSpec((1, d, tn), lambda l, j: (l, 0, j)),
            pl.BlockSpec((1, 1, tn), lambda l, j: (l, 0, j)),
        ],
        out_specs=pl.BlockSpec((1, rows, tn), lambda l, j: (l, 0, j)),
        compiler_params=_params("parallel", "parallel"),
        name="adaln_modulation",
    )(c_rows, w_mod, b_mod.reshape(depth, 1, n))


def _mod_spec(lay, tm, layer, part, d, m_axis=0):
    def index(*ids):
        return (layer * MOD_ROWS + _seq_index(ids[m_axis] * tm, lay), 0, part)
    return pl.BlockSpec((1, 1, d), index)


def _norm_modulate(x, gain, scale, shift):
    ms = jnp.mean(x * x, axis=-1, keepdims=True)
    y = x * lax.rsqrt(ms + EPS) * gain
    return y * (1.0 + scale) + shift


def _in_proj_body(x_ref, sh_ref, sc_ref, gain_ref, w_ref, o_ref):
    h = _norm_modulate(x_ref[...], gain_ref[...], sc_ref[0], sh_ref[0])
    o_ref[...] = jnp.dot(h.astype(BF16), w_ref[...], preferred_element_type=F32).astype(o_ref.dtype)


def _in_proj(x, mod3, gain, w, lay, layer, out_dtype):
    t, d = x.shape
    n = w.shape[1]
    tm = _row_tile(lay, 512)
    tn = _tile(n, n // 2 if n % (2 * V7X_MXU_COLS) == 0 else n, V7X_LANES)
    return pl.pallas_call(
        _in_proj_body,
        out_shape=jax.ShapeDtypeStruct((t, n), out_dtype),
        grid=(n // tn, t // tm),
        in_specs=[
            pl.BlockSpec((tm, d), lambda j, m: (m, 0)),
            _mod_spec(lay, tm, layer, 0, d, m_axis=1),
            _mod_spec(lay, tm, layer, 1, d, m_axis=1),
            pl.BlockSpec((1, d), lambda j, m: (0, 0)),
            pl.BlockSpec((d, tn), lambda j, m: (0, j), pipeline_mode=pl.Buffered(1)),
        ],
        out_specs=pl.BlockSpec((tm, tn), lambda j, m: (m, j)),
        compiler_params=_params("parallel", "parallel"),
        name="mixer_in_proj",
    )(x, mod3, mod3, gain.reshape(1, d), w)


def _gates(z, one_minus_lb):
    t = jnp.exp2(jnp.abs(z) * (-LOG2_E))
    r = 1.0 / (1.0 + t)
    k = one_minus_lb * jnp.where(z >= 0, t * r, r)
    log2f = jnp.log2(jnp.maximum(1.0 - k, MIN_FORGET))
    return log2f, k


def _hg_levels(c, leaf):
    widths = []
    w = leaf
    while w < c:
        widths.append(w)
        w *= 2
    return widths


def _hg_wide_constants(c, leaf, reverse):
    t = np.arange(c)[:, None]
    s = np.arange(c)[None, :]
    if reverse:
        t, s = c - 1 - t, c - 1 - s
    cum = (s <= t).astype(np.float32)
    level = np.full((c, c), -1, np.int32)
    level[(s <= t) & (t // leaf == s // leaf)] = 0
    for i, w in enumerate(_hg_levels(c, leaf)):
        level[(s <= t) & (t // (2 * w) == s // (2 * w)) & (t // w != s // w)] = i + 1
    n_levels = 1 + len(_hg_levels(c, leaf))
    wide = np.full((c, n_levels * c), -1, np.int32)
    for l in range(n_levels):
        wide[:, l * c:(l + 1) * c] = np.where(level == l, l, -1)
    return cum, wide, n_levels


def _hg_reference_rows(e_cum, leaf, reverse):
    c, width = e_cum.shape
    mid = leaf // 2 if reverse else leaf // 2 - 1
    refs = [jnp.concatenate(
        [jnp.broadcast_to(e_cum[b + mid:b + mid + 1, :], (leaf, width)) for b in range(0, c, leaf)], axis=0)]
    for w in _hg_levels(c, leaf):
        row = w if reverse else w - 1
        refs.append(jnp.concatenate(
            [jnp.broadcast_to(e_cum[p + row:p + row + 1, :], (2 * w, width)) for p in range(0, c, 2 * w)],
            axis=0))
    return refs


def _hgrn_wide_body(lay, nb, n_levels, n_g, reverse, with_cast, *refs):
    if with_cast:
        *refs, o_ref, w16_ref, state_scr = refs
        *refs, w32_ref = refs
        w16_ref[...] = w32_ref[...].astype(BF16)
    else:
        *refs, o_ref, state_scr = refs
    if reverse:
        q_ref, z_ref, v_ref, g_ref, fwd_ref, lb_ref, gain_ref, cum_ref, wl_ref = refs
    else:
        q_ref, z_ref, v_ref, lb_ref, cum_ref, wl_ref = refs
    step = pl.program_id(1)
    c = HG_CHUNK
    n_chunks = q_ref.shape[0] // c
    width = n_g * HEAD_DIM
    blk = (nb - 1 - step) if reverse else step
    row0 = blk * (n_chunks * c)
    bounds = [st + ln for st, ln in zip(lay.starts, lay.lens)] if reverse else list(lay.starts)
    edge = row0 + n_chunks * c if reverse else row0
    reset = functools.reduce(jnp.logical_or, [edge == b for b in bounds])

    @pl.when(reset)
    def _():
        state_scr[...] = jnp.zeros_like(state_scr)

    nt = (((1,), (1,)), ((), ()))
    tn = (((0,), (0,)), ((), ()))
    last_row = 0 if reverse else c - 1

    def chunk(j, carry):
        jj = (n_chunks - 1 - j) if reverse else j
        rows = pl.ds(pl.multiple_of(jj * c, c), c)
        q = q_ref[rows, :].astype(F32)
        v16 = v_ref[rows, :].astype(BF16)
        states = [state_scr[h] for h in range(n_g)]
        log2f, k = _gates(z_ref[rows, :].astype(F32), 1.0 - lb_ref[...])
        hi = log2f.astype(BF16)
        lo = (log2f - hi.astype(F32)).astype(BF16)
        e2 = jnp.dot(cum_ref[...], jnp.concatenate([hi, lo], axis=1), preferred_element_type=F32)
        e_cum = e2[:, :width] + e2[:, width:]
        e_last = e_cum[last_row:last_row + 1, :]
        refs_l = _hg_reference_rows(e_cum, HG_LEAF, reverse)
        e_leaf = e_cum - refs_l[0]
        q_parts = [q * jnp.exp2(e_leaf)]
        k_parts = [k * jnp.exp2(-e_leaf)]
        for ref in refs_l[1:]:
            x = jnp.exp2(-jnp.abs(e_cum - ref))
            q_parts.append(q * x)
            k_parts.append(k * x)
        q_in = (q * jnp.exp2(e_cum)).astype(BF16)
        k_out = (k * jnp.exp2(e_last - e_cum)).astype(BF16)
        state_decay = jnp.exp2(e_last)
        wide_level = wl_ref[...]
        lane_blocks = [slice(b, b + V7X_LANES) for b in range(0, n_levels * c, V7X_LANES)]
        masks = [[(wide_level[:, lb_] == l) for l in range(lb_.start // c, lb_.stop // c)] for lb_ in lane_blocks]
        outs = []
        for h in range(n_g):
            cols = slice(h * HEAD_DIM, (h + 1) * HEAD_DIM)
            q_stack = jnp.concatenate([p[:, cols] for p in q_parts], axis=0).astype(BF16)
            k_stack = jnp.concatenate([p[:, cols] for p in k_parts], axis=0).astype(BF16)
            r = lax.dot_general(q_stack, k_stack, nt, preferred_element_type=F32)
            pieces = []
            for lb_, lane_masks in zip(lane_blocks, masks):
                piece = jnp.zeros((c, V7X_LANES), F32)
                for l, mask in zip(range(lb_.start // c, lb_.stop // c), lane_masks):
                    piece = jnp.where(mask, r[l * c:(l + 1) * c, lb_], piece)
                pieces.append(piece)
            scores = jnp.concatenate(pieces, axis=1)
            v_h = v16[:, cols]
            o = jnp.dot(scores.astype(BF16), jnp.concatenate([v_h] * n_levels, axis=0),
                        preferred_element_type=F32)
            o = o + lax.dot_general(q_in[:, cols], states[h].astype(BF16), nt, preferred_element_type=F32)
            upd = lax.dot_general(v_h, k_out[:, cols], tn, preferred_element_type=F32)
            states[h] = states[h] * state_decay[:, cols] + upd
            outs.append(o)
        for h in range(n_g):
            state_scr[h] = states[h]
        if reverse:
            g = g_ref[rows, :].astype(F32)
            fwd = fwd_ref[rows, :]
            for h in range(n_g):
                cols = slice(h * HEAD_DIM, (h + 1) * HEAD_DIM)
                tot = fwd[:, cols] + outs[h]
                ms = jnp.mean(tot * tot, axis=-1, keepdims=True)
                y = tot * lax.rsqrt(ms + EPS) * gain_ref[...]
                o_ref[rows, cols] = (y * _silu(g[:, cols])).astype(o_ref.dtype)
        else:
            for h in range(n_g):
                o_ref[rows, h * HEAD_DIM:(h + 1) * HEAD_DIM] = outs[h]
        return carry

    lax.fori_loop(0, n_chunks, chunk, 0, unroll=HG_UNROLL)


def _hgrn_dir(proj, z_group, lb, fwd_out, out_gain, lay, n_heads, reverse, out_dtype, cast=None):
    t = proj.shape[0]
    lb_rows = _tile(int(np.gcd.reduce(lay.lens)), HG_BLOCK, HG_CHUNK)
    nb = t // lb_rows
    n_g = min(n_heads, HG_HEADS_PER_STEP)
    assert n_heads % n_g == 0
    n_groups = n_heads // n_g
    mats, level, n_levels = _hg_wide_constants(HG_CHUNK, HG_LEAF, reverse)
    width = n_g * HEAD_DIM

    def col(group):
        return lambda g, s: ((nb - 1 - s) if reverse else s, group * n_groups + g)

    blk = (lb_rows, width)
    const = lambda g, s: (0, 0)
    in_specs = [pl.BlockSpec(blk, col(0)), pl.BlockSpec(blk, col(z_group)), pl.BlockSpec(blk, col(3))]
    args = [proj, proj, proj]
    if reverse:
        in_specs += [pl.BlockSpec(blk, col(4)), pl.BlockSpec(blk, col(0))]
        args += [proj, fwd_out]
    in_specs.append(pl.BlockSpec((1, width), lambda g, s: (0, g)))
    args.append(lb.reshape(1, n_heads * HEAD_DIM))
    if reverse:
        in_specs.append(pl.BlockSpec((1, HEAD_DIM), const))
        args.append(out_gain.reshape(1, HEAD_DIM))
    in_specs += [pl.BlockSpec(mats.shape, const), pl.BlockSpec(level.shape, const)]
    args += [jnp.asarray(mats, BF16), jnp.asarray(level)]
    out_specs = pl.BlockSpec(blk, col(0))
    out_shape = jax.ShapeDtypeStruct((t, n_heads * HEAD_DIM), out_dtype)
    if cast is not None:
        src, first_row, n_rows, col_block, n_cols = cast
        assert n_groups == 1 and n_rows % (16 * nb) == 0 and first_row % (n_rows // nb) == 0
        chunk = n_rows // nb
        in_specs.append(pl.BlockSpec((chunk, n_cols), lambda g, s: (first_row // chunk + s, col_block)))
        args.append(src)
        out_specs = (out_specs, pl.BlockSpec((chunk, n_cols), lambda g, s: (s, 0)))
        out_shape = (out_shape, jax.ShapeDtypeStruct((n_rows, n_cols), BF16))
    return pl.pallas_call(
        functools.partial(_hgrn_wide_body, lay, nb, n_levels, n_g, reverse, cast is not None),
        out_shape=out_shape,
        grid=(n_groups, nb),
        in_specs=in_specs,
        out_specs=out_specs,
        scratch_shapes=[pltpu.VMEM((n_g, HEAD_DIM, HEAD_DIM), F32)],
        compiler_params=_params("parallel", "arbitrary"),
        name="hgrn2_bwd" if reverse else "hgrn2_fwd",
    )(*args)


def _hgrn(proj, lbs_f, lbs_b, out_gain, lay, n_heads, out_dtype, cast_fwd=None, cast_bwd=None):
    fwd = _hgrn_dir(proj, 1, lbs_f, None, None, lay, n_heads, False, F32, cast_fwd)
    o_fwd, w_a = fwd if cast_fwd is not None else (fwd, None)
    bwd = _hgrn_dir(proj, 2, lbs_b, o_fwd, out_gain, lay, n_heads, True, out_dtype, cast_bwd)
    o, w_b = bwd if cast_bwd is not None else (bwd, None)
    return o, w_a, w_b


def _rope_tables(n_pos):
    n_freq = HEAD_DIM // 4
    pos = jnp.arange(n_pos)
    coords = jnp.stack([pos // GRID_W, pos % GRID_W], axis=-1).astype(F32)
    inv = ROPE_THETA ** (-jnp.arange(n_freq, dtype=F32) / n_freq)
    ang = coords[:, :, None] * inv
    cos = jnp.broadcast_to(jnp.cos(ang)[:, :, None, :], (n_pos, 2, 2, n_freq))
    sin = jnp.sin(ang)
    sin = jnp.stack([-sin, sin], axis=2)
    return cos.reshape(n_pos, HEAD_DIM), sin.reshape(n_pos, HEAD_DIM)


def _qk_prep_body(n_q, n_kv, q_ref, k_ref, v_ref, cos_ref, sin_ref, qg_ref, kg_ref, swap_ref,
                  qo_ref, ko_ref, vo_ref):
    cos = cos_ref[...]
    sin = sin_ref[...]
    swap = swap_ref[...]

    def norm_rope(x, gain, scale):
        ms = jnp.mean(x * x, axis=-1, keepdims=True)
        y = x * lax.rsqrt(ms + EPS) * gain
        partner = jnp.dot(y.astype(BF16), swap, preferred_element_type=F32)
        out = y * cos + partner * sin
        return out * scale if scale is not None else out

    for h in range(n_q):
        cols = slice(h * HEAD_DIM, (h + 1) * HEAD_DIM)
        qo_ref[:, cols] = norm_rope(q_ref[:, cols].astype(F32), qg_ref[...],
                                    HEAD_DIM ** -0.5 * LOG2_E).astype(qo_ref.dtype)
    for h in range(n_kv):
        cols = slice(h * HEAD_DIM, (h + 1) * HEAD_DIM)
        ko_ref[:, cols] = norm_rope(k_ref[:, cols].astype(F32), kg_ref[...], None).astype(ko_ref.dtype)
    vo_ref[...] = v_ref[...].astype(vo_ref.dtype)


def _qk_prep(proj, q_gain, k_gain, cos, sin, lay, q_col, n_q, n_kv):
    t = proj.shape[0]
    tm = _row_tile(lay, 256)
    qw, kw = n_q * HEAD_DIM, n_kv * HEAD_DIM
    assert q_col % qw == 0 and (q_col + qw) % kw == 0

    def pos_block(m):
        row = m * tm
        return ((row - _seq_start(row, lay)) // tm, 0)

    quarter = HEAD_DIM // 4
    lane = np.arange(HEAD_DIM)
    source = np.where(lane % (2 * quarter) < quarter, lane + quarter, lane - quarter)
    swap = (lane[:, None] == source[None, :]).astype(np.float32)

    return pl.pallas_call(
        functools.partial(_qk_prep_body, n_q, n_kv),
        out_shape=(jax.ShapeDtypeStruct((t, qw), BF16), jax.ShapeDtypeStruct((t, kw), BF16),
                   jax.ShapeDtypeStruct((t, kw), BF16)),
        grid=(t // tm,),
        in_specs=[
            pl.BlockSpec((tm, qw), lambda m: (m, q_col // qw)),
            pl.BlockSpec((tm, kw), lambda m: (m, (q_col + qw) // kw)),
            pl.BlockSpec((tm, kw), lambda m: (m, (q_col + qw) // kw + 1)),
            pl.BlockSpec((tm, HEAD_DIM), pos_block),
            pl.BlockSpec((tm, HEAD_DIM), pos_block),
            pl.BlockSpec((1, HEAD_DIM), lambda m: (0, 0)),
            pl.BlockSpec((1, HEAD_DIM), lambda m: (0, 0)),
            pl.BlockSpec((HEAD_DIM, HEAD_DIM), lambda m: (0, 0)),
        ],
        out_specs=(pl.BlockSpec((tm, qw), lambda m: (m, 0)), pl.BlockSpec((tm, kw), lambda m: (m, 0)),
                   pl.BlockSpec((tm, kw), lambda m: (m, 0))),
        compiler_params=_params("parallel"),
        name="attn_qk_prep",
    )(proj, proj, proj, cos, sin, q_gain.reshape(1, HEAD_DIM), k_gain.reshape(1, HEAD_DIM),
      jnp.asarray(swap, BF16))


def _flash_body(with_cast, qb_ref, kb_ref, flag_ref, q_ref, k_ref, v_ref, *rest):
    if with_cast:
        w32_ref, o_ref, w16_ref, m_scr, l_scr, acc_scr = rest
        w16_ref[...] = w32_ref[...].astype(BF16)
    else:
        o_ref, m_scr, l_scr, acc_scr = rest
    i = pl.program_id(1)
    flags = flag_ref[i]
    tq = q_ref.shape[0]

    @pl.when((flags & 1) != 0)
    def _():
        m_scr[...] = jnp.full_like(m_scr, -jnp.inf)
        l_scr[...] = jnp.zeros_like(l_scr)
        acc_scr[...] = jnp.zeros_like(acc_scr)

    q = q_ref[...]
    qs = jnp.concatenate([q[:, g * HEAD_DIM:(g + 1) * HEAD_DIM] for g in range(ATT_GROUP)], axis=0)
    s = lax.dot_general(qs, k_ref[...], (((1,), (1,)), ((), ())), preferred_element_type=F32)
    m_prev = m_scr[...]
    m_new = jnp.maximum(m_prev, jnp.max(s, axis=-1, keepdims=True))
    alpha = jnp.exp2(m_prev - m_new)
    p = jnp.exp2(s - m_new)
    l_scr[...] = alpha * l_scr[...] + jnp.sum(p, axis=-1, keepdims=True)
    acc_scr[...] = alpha * acc_scr[...] + jnp.dot(p.astype(BF16), v_ref[...], preferred_element_type=F32)
    m_scr[...] = m_new

    @pl.when((flags & 2) != 0)
    def _():
        out = acc_scr[...] / l_scr[...]
        for g in range(ATT_GROUP):
            o_ref[:, g * HEAD_DIM:(g + 1) * HEAD_DIM] = out[g * tq:(g + 1) * tq].astype(o_ref.dtype)


def _flash_attention(q_rot, k_rot, v16, lay, n_kv, out_dtype, cast_src=None, cast_unit=0, cast_units=1):
    t = q_rot.shape[0]
    g = int(np.gcd.reduce(lay.lens))
    tq = _tile(g, 256, 16)
    tk = _tile(g, FLASH_KV_ROWS, V7X_LANES)
    qb, kb, flags = [], [], []
    for st, ln in zip(lay.starts, lay.lens):
        for qi in range(ln // tq):
            for ki in range(ln // tk):
                qb.append(st // tq + qi)
                kb.append(st // tk + ki)
                flags.append((1 if ki == 0 else 0) | (2 if ki == ln // tk - 1 else 0))
    n_items = len(qb)
    gw = ATT_GROUP * HEAD_DIM
    in_specs = [
        pl.BlockSpec((tq, gw), lambda h, i, qb, kb, fl: (qb[i], h)),
        pl.BlockSpec((tk, HEAD_DIM), lambda h, i, qb, kb, fl: (kb[i], h)),
        pl.BlockSpec((tk, HEAD_DIM), lambda h, i, qb, kb, fl: (kb[i], h)),
    ]
    out_specs = pl.BlockSpec((tq, gw), lambda h, i, qb, kb, fl: (qb[i], h))
    out_shape = jax.ShapeDtypeStruct((t, n_kv * gw), out_dtype)
    args = [q_rot, k_rot, v16]
    if cast_src is not None:
        rows, cols = cast_src.shape[0] // cast_units, cast_src.shape[1]
        n_steps = n_kv * n_items
        chunk = min(c for c in range(16, rows + 1, 16) if rows % c == 0 and rows // c <= n_steps)
        n_chunks = rows // chunk

        def chunk_of(h, i):
            return jnp.minimum(h * n_items + i, n_chunks - 1)

        in_specs.append(pl.BlockSpec((chunk, cols),
                                     lambda h, i, qb, kb, fl: (cast_unit * n_chunks + chunk_of(h, i), 0)))
        out_specs = (out_specs, pl.BlockSpec((chunk, cols), lambda h, i, qb, kb, fl: (chunk_of(h, i), 0)))
        out_shape = (out_shape, jax.ShapeDtypeStruct((rows, cols), BF16))
        args.append(cast_src)
    grid_spec = pltpu.PrefetchScalarGridSpec(
        num_scalar_prefetch=3,
        grid=(n_kv, n_items),
        in_specs=in_specs,
        out_specs=out_specs,
        scratch_shapes=[pltpu.VMEM((ATT_GROUP * tq, 1), F32), pltpu.VMEM((ATT_GROUP * tq, 1), F32),
                        pltpu.VMEM((ATT_GROUP * tq, HEAD_DIM), F32)],
    )
    return pl.pallas_call(
        functools.partial(_flash_body, cast_src is not None),
        out_shape=out_shape,
        grid_spec=grid_spec,
        compiler_params=_params("arbitrary", "arbitrary"),
        name="attn_flash",
    )(jnp.asarray(qb, jnp.int32), jnp.asarray(kb, jnp.int32), jnp.asarray(flags, jnp.int32), *args)


def _out_proj_body(a_ref, b_ref, wa_ref, wb_ref, x_ref, g_ref, o_ref):
    acc = jnp.dot(a_ref[...], wa_ref[...], preferred_element_type=F32)
    acc = acc + jnp.dot(b_ref[...], wb_ref[...], preferred_element_type=F32)
    o_ref[...] = x_ref[...] + g_ref[0] * acc


def _out_proj(o_hg, o_att, w_out, x, mod3, lay, layer):
    t, d = x.shape
    ka, kb = o_hg.shape[1], o_att.shape[1]
    tm = _row_tile(lay, 512)
    return pl.pallas_call(
        _out_proj_body,
        out_shape=jax.ShapeDtypeStruct((t, d), F32),
        grid=(t // tm,),
        in_specs=[
            pl.BlockSpec((tm, ka), lambda m: (m, 0)),
            pl.BlockSpec((tm, kb), lambda m: (m, 0)),
            pl.BlockSpec((ka, d), lambda m: (0, 0)),
            pl.BlockSpec((kb, d), lambda m: (1, 0)),
            pl.BlockSpec((tm, d), lambda m: (m, 0)),
            _mod_spec(lay, tm, layer, 2, d),
        ],
        out_specs=pl.BlockSpec((tm, d), lambda m: (m, 0)),
        compiler_params=_params("parallel"),
        name="mixer_out_proj",
    )(o_hg, o_att, w_out, w_out, x, mod3)


def _router_body(x_ref, sh_ref, sc_ref, gain_ref, wr_ref, o_ref, idx_ref, gate_ref):
    h = _norm_modulate(x_ref[...], gain_ref[...], sc_ref[0], sh_ref[0])
    o_ref[...] = h.astype(o_ref.dtype)
    logits = jnp.dot(h, wr_ref[...], preferred_element_type=F32, precision=lax.Precision.HIGHEST)
    n_e = logits.shape[1]
    lane = lax.broadcasted_iota(jnp.int32, logits.shape, 1)
    m1 = jnp.max(logits, axis=-1, keepdims=True)
    i1 = jnp.min(jnp.where(logits == m1, lane, n_e), axis=-1, keepdims=True)
    rest = jnp.where(lane == i1, -jnp.inf, logits)
    m2 = jnp.max(rest, axis=-1, keepdims=True)
    i2 = jnp.min(jnp.where(rest == m2, lane, n_e), axis=-1, keepdims=True)
    e = jnp.exp(m2 - m1)
    tot = 1.0 + e
    two = lax.broadcasted_iota(jnp.int32, idx_ref.shape, 1)
    idx_ref[...] = jnp.where(two == 0, i1, i2)
    gate_ref[...] = jnp.where(two == 0, 1.0 / tot, e / tot)


def _router(x, mod3, gain, lay, layer, w_router):
    t, d = x.shape
    tm = _row_tile(lay, 256)
    in_specs = [
        pl.BlockSpec((tm, d), lambda m: (m, 0)),
        _mod_spec(lay, tm, layer, 3, d),
        _mod_spec(lay, tm, layer, 4, d),
        pl.BlockSpec((1, d), lambda m: (0, 0)),
    ]
    row_spec = pl.BlockSpec((tm, d), lambda m: (m, 0))
    n_e = w_router.shape[1]
    small = pl.BlockSpec((tm, TOP_K), lambda m: (m, 0))
    return pl.pallas_call(
        _router_body,
        out_shape=(jax.ShapeDtypeStruct((t, d), F32), jax.ShapeDtypeStruct((t, TOP_K), jnp.int32),
                   jax.ShapeDtypeStruct((t, TOP_K), F32)),
        grid=(t // tm,),
        in_specs=in_specs + [pl.BlockSpec((d, n_e), lambda m: (0, 0))],
        out_specs=(row_spec, small, small),
        compiler_params=_params("parallel"), name="moe_router",
    )(x, mod3, mod3, gain.reshape(1, d), w_router)


def _swiglu_step(x16, wg_ref, wu_ref, wd_refs, acc_scr):
    gate = jnp.dot(x16, wg_ref[0], preferred_element_type=F32)
    up = jnp.dot(x16, wu_ref[0], preferred_element_type=F32)
    act = (_silu(gate) * up).astype(BF16)
    col = 0
    for wd_ref in wd_refs:
        n = wd_ref.shape[2]
        acc_scr[:, col:col + n] += jnp.dot(act, wd_ref[0], preferred_element_type=F32)
        col += n


def _dense_ffn_body(x_ref, sh_ref, sc_ref, gain_ref, wg_ref, wu_ref, wd_ref, g_ref, o_ref, x16_scr, acc_scr):
    f = pl.program_id(1)

    @pl.when(f == 0)
    def _():
        x16_scr[...] = _norm_modulate(x_ref[...], gain_ref[...], sc_ref[0], sh_ref[0]).astype(BF16)
        acc_scr[...] = jnp.zeros_like(acc_scr)

    _swiglu_step(x16_scr[...], wg_ref, wu_ref, (wd_ref,), acc_scr)

    @pl.when(f == pl.num_programs(1) - 1)
    def _():
        o_ref[...] = x_ref[...] + g_ref[0] * acc_scr[...]


def _dense_ffn(x, mod3, gain, w_gate, w_up, w_down, j, lay, layer):
    t, d = x.shape
    fdim = w_gate.shape[2]
    tm = _row_tile(lay, 512)
    tf = _tile(fdim, FFN_COLS, V7X_LANES)
    return pl.pallas_call(
        _dense_ffn_body,
        out_shape=jax.ShapeDtypeStruct((t, d), F32),
        grid=(t // tm, fdim // tf),
        in_specs=[
            pl.BlockSpec((tm, d), lambda m, f: (m, 0)),
            _mod_spec(lay, tm, layer, 3, d),
            _mod_spec(lay, tm, layer, 4, d),
            pl.BlockSpec((1, d), lambda m, f: (0, 0)),
            pl.BlockSpec((1, d, tf), lambda m, f: (j, 0, f)),
            pl.BlockSpec((1, d, tf), lambda m, f: (j, 0, f)),
            pl.BlockSpec((1, tf, d), lambda m, f: (j, f, 0)),
            _mod_spec(lay, tm, layer, 5, d),
        ],
        out_specs=pl.BlockSpec((tm, d), lambda m, f: (m, 0)),
        scratch_shapes=[pltpu.VMEM((tm, d), BF16), pltpu.VMEM((tm, d), F32)],
        compiler_params=_params("parallel", "arbitrary"),
        name="swiglu_residual",
    )(x, mod3, mod3, gain.reshape(1, d), w_gate, w_up, w_down, mod3)


def _expert_ffn_body(n_f, n_down, be_ref, nu_ref, tok_ref, h_hbm, wg_ref, wu_ref, *rest):
    wd_refs = rest[:n_down]
    o_ref, xbuf, x16_scr, acc_scr, sems = rest[n_down:]
    m = pl.program_id(0)
    f = pl.program_id(1)
    tm = x16_scr.shape[0]
    n_used = nu_ref[0]
    used = m < n_used
    last = f == n_f - 1

    def row_copy(src_row, slot, dst_row):
        return pltpu.make_async_copy(h_hbm.at[pl.ds(src_row, 1)], xbuf.at[slot, pl.ds(dst_row, 1)],
                                     sems.at[slot])

    slot = m % 2
    share = tm // n_f
    extra = tm - share * n_f

    @pl.when(jnp.logical_and(used, f == 0))
    def _():
        @pl.when(m == 0)
        def _():
            def issue(r, carry):
                row_copy(tok_ref[r], 0, r).start()
                return carry
            lax.fori_loop(0, tm, issue, 0)

        for _ in range(tm):
            row_copy(0, slot, 0).wait()
        x16_scr[...] = xbuf[slot].astype(BF16)
        acc_scr[...] = jnp.zeros_like(acc_scr)

    first = (m + 1) * tm

    @pl.when(used)
    def _():
        for r in range(share):
            row = f * share + r
            row_copy(tok_ref[first + row], 1 - slot, row).start()
        _swiglu_step(x16_scr[...], wg_ref, wu_ref, wd_refs, acc_scr)

    @pl.when(jnp.logical_and(used, last))
    def _():
        for r in range(share * n_f, share * n_f + extra):
            row_copy(tok_ref[first + r], 1 - slot, r).start()
        o_ref[...] = acc_scr[...]

        @pl.when(m + 1 == n_used)
        def _():
            for _ in range(tm):
                row_copy(0, 1 - slot, 0).wait()

    @pl.when(jnp.logical_and(jnp.logical_not(used), last))
    def _():
        o_ref[...] = jnp.zeros_like(o_ref)


def _expert_ffn(h32, row_token, w_gate, w_up, w_down_slabs, block_expert, n_used, tm):
    d = h32.shape[1]
    r = row_token.shape[0]
    fdim = w_gate.shape[2]
    tf = _tile(fdim, FFN_COLS, V7X_LANES)
    nf = fdim // tf

    def fcol(m, f, nu):
        return jnp.where(m < nu[0], f, nf - 1)

    grid_spec = pltpu.PrefetchScalarGridSpec(
        num_scalar_prefetch=3,
        grid=(r // tm, nf),
        in_specs=[
            pl.BlockSpec(memory_space=pl.ANY),
            pl.BlockSpec((1, d, tf), lambda m, f, be, nu, tok: (be[m], 0, fcol(m, f, nu))),
            pl.BlockSpec((1, d, tf), lambda m, f, be, nu, tok: (be[m], 0, fcol(m, f, nu))),
        ] + [pl.BlockSpec((1, tf, w.shape[2]), lambda m, f, be, nu, tok: (be[m], fcol(m, f, nu), 0))
             for w in w_down_slabs],
        out_specs=pl.BlockSpec((tm, d), lambda m, f, be, nu, tok: (m, 0)),
        scratch_shapes=[pltpu.VMEM((2, tm, d), F32), pltpu.VMEM((tm, d), BF16), pltpu.VMEM((tm, d), F32),
                        pltpu.SemaphoreType.DMA((2,))],
    )
    return pl.pallas_call(
        functools.partial(_expert_ffn_body, nf, len(w_down_slabs)),
        out_shape=jax.ShapeDtypeStruct((r, d), F32),
        grid_spec=grid_spec,
        compiler_params=_params("arbitrary", "arbitrary"),
        name="swiglu_experts",
    )(block_expert, n_used, row_token, h32, w_gate, w_up, *w_down_slabs)


def _combine_body(dest_ref, yb_hbm, x_ref, gate_ref, g_ref, o_ref, buf, sems):
    tm = x_ref.shape[0]
    i = pl.program_id(0)
    slot = i % 2

    def row_copy(src_row, slot_, k, dst_row):
        return pltpu.make_async_copy(yb_hbm.at[pl.ds(src_row, 1)], buf.at[slot_, k, pl.ds(dst_row, 1)],
                                     sems.at[slot_])

    def gather(tile, slot_):
        def issue(r, carry):
            for k in range(TOP_K):
                row_copy(dest_ref[(tile * tm + r) * TOP_K + k], slot_, k, r).start()
            return carry
        lax.fori_loop(0, tm, issue, 0)

    @pl.when(i == 0)
    def _():
        gather(0, 0)

    @pl.when(i + 1 < pl.num_programs(0))
    def _():
        gather(i + 1, 1 - slot)

    for _ in range(tm * TOP_K):
        row_copy(0, slot, 0, 0).wait()
    gates = gate_ref[...]
    y = buf[slot, 0] * gates[:, 0:1]
    for k in range(1, TOP_K):
        y = y + buf[slot, k] * gates[:, k:k + 1]
    o_ref[...] = x_ref[...] + g_ref[0] * y


def _combine(yb, dest, gates, x, mod3, lay, layer):
    t, d = x.shape
    tm = _row_tile(lay, 256)
    grid_spec = pltpu.PrefetchScalarGridSpec(
        num_scalar_prefetch=1,
        grid=(t // tm,),
        in_specs=[
            pl.BlockSpec(memory_space=pl.ANY),
            pl.BlockSpec((tm, d), lambda m, dest: (m, 0)),
            pl.BlockSpec((tm, TOP_K), lambda m, dest: (m, 0)),
            _mod_spec(lay, tm, layer, 5, d),
        ],
        out_specs=pl.BlockSpec((tm, d), lambda m, dest: (m, 0)),
        scratch_shapes=[pltpu.VMEM((2, TOP_K, tm, d), yb.dtype), pltpu.SemaphoreType.DMA((2,))],
    )
    return pl.pallas_call(
        _combine_body,
        out_shape=jax.ShapeDtypeStruct((t, d), F32),
        grid_spec=grid_spec,
        compiler_params=_params("arbitrary"),
        name="moe_combine",
    )(dest, yb, x, gates, mod3)


def _moe(x, h32, idx, gates, w_gate, w_up, w_down_slabs, n_e, mod3, lay, layer):
    t, d = x.shape
    tm = _row_tile(lay, 512)
    n_slots = t * TOP_K
    cap = n_slots + n_e * tm
    e = idx.reshape(-1)
    onehot = (e[:, None] == jnp.arange(n_e, dtype=jnp.int32)[None, :]).astype(jnp.int32)
    csum = jnp.cumsum(onehot, axis=0)
    rank = jnp.take_along_axis(csum, e[:, None], axis=1)[:, 0] - 1
    counts = csum[-1]
    padded = (counts + tm - 1) // tm * tm
    pad_end = jnp.cumsum(padded)
    pad_start = pad_end - padded
    dest = (pad_start[e] + rank).astype(jnp.int32)
    row_token = jnp.zeros((cap,), jnp.int32).at[dest].set(jnp.arange(n_slots, dtype=jnp.int32) // TOP_K)
    block_first = jnp.arange(cap // tm, dtype=jnp.int32) * tm
    block_e = jnp.minimum(jnp.searchsorted(pad_end, block_first, side='right'), n_e - 1).astype(jnp.int32)
    n_used = (pad_end[-1:] // tm).astype(jnp.int32)
    block_e = jnp.where(block_first < pad_end[-1], block_e, block_e[jnp.maximum(n_used[0] - 1, 0)])
    yb = _expert_ffn(h32, row_token, w_gate, w_up, w_down_slabs, block_e, n_used, tm)
    return _combine(yb, dest, gates, x, mod3, lay, layer)


def _final_norm_body(x_ref, gain_ref, o_ref):
    x = x_ref[...]
    ms = jnp.mean(x * x, axis=-1, keepdims=True)
    o_ref[...] = x * lax.rsqrt(ms + EPS) * gain_ref[...]


def _final_norm(x, gain, row0, n_rows, tm):
    d = x.shape[1]
    first = row0 // tm
    return pl.pallas_call(
        _final_norm_body,
        out_shape=jax.ShapeDtypeStruct((n_rows, d), F32),
        grid=(n_rows // tm,),
        in_specs=[pl.BlockSpec((tm, d), lambda m: (first + m, 0)), pl.BlockSpec((1, d), lambda m: (0, 0))],
        out_specs=pl.BlockSpec((tm, d), lambda m: (m, 0)),
        compiler_params=_params("parallel"),
        name="final_norm",
    )(x, gain.reshape(1, d))


def kernel(x_prompt, x_sample, c_prompt, c_sample, w_mod, b_mod, norm_mix, norm_ffn, w_in, hg_lb_logits,
           hg_out_norm, qk_norm, w_out, ffn_gate, ffn_up, ffn_down, router, exp_gate, exp_up, exp_down,
           final_norm):
    depth, d = norm_mix.shape
    lay = _layout(x_prompt, x_sample)
    n_seq = len(lay.starts)
    assert n_seq <= MOD_ROWS
    hg_heads = hg_lb_logits.shape[2] // HEAD_DIM
    kv_heads = (w_in.shape[2] - 5 * hg_heads * HEAD_DIM) // HEAD_DIM // (ATT_GROUP + 2)
    att_heads = kv_heads * ATT_GROUP
    hg_width = hg_heads * HEAD_DIM
    n_e = exp_gate.shape[1]

    x = jnp.concatenate([x_prompt.reshape(-1, d), x_sample.reshape(-1, d)], axis=0)
    c_rows = jnp.concatenate([c_prompt, c_sample, jnp.zeros((MOD_ROWS - n_seq, d), F32)], axis=0)
    mod3 = _modulation(c_rows, w_mod, b_mod).reshape(depth * MOD_ROWS, 1, 6 * d)
    lbs = _lower_bounds(hg_lb_logits)
    cos, sin = _rope_tables(max(lay.lens))

    w_in16 = w_in.astype(BF16)
    w_out16 = w_out.astype(BF16)
    dense16 = [w.astype(BF16) for w in (ffn_gate, ffn_up, ffn_down)]
    n_moe = exp_gate.shape[0]
    assert depth == 2 * n_moe, "one attention call per (MoE layer, gate / up) weight slab"
    exp_gate2 = exp_gate.reshape(-1, exp_gate.shape[-1])
    exp_up2 = exp_up.reshape(-1, exp_up.shape[-1])
    exp_down2 = exp_down.reshape(-1, d)
    down_rows = n_e * exp_down.shape[2]
    expert16 = [[None] * n_moe, [None] * n_moe]
    down16 = [None] * n_moe

    for layer in range(depth):
        proj = _in_proj(x, mod3, norm_mix[layer], w_in16[layer], lay, layer, PROJ_DTYPE)
        if layer % 2 == 0:
            casts = [(exp_down2, (layer // 2) * down_rows, down_rows, half, d // 2) for half in range(2)]
            o_hg, *slabs = _hgrn(proj, lbs[0, layer], lbs[1, layer], hg_out_norm[layer], lay, hg_heads, BF16,
                                 *casts)
            down16[layer // 2] = [s.reshape(n_e, -1, d // 2) for s in slabs]
        else:
            o_hg, _, _ = _hgrn(proj, lbs[0, layer], lbs[1, layer], hg_out_norm[layer], lay, hg_heads, BF16)
        q_rot, k_rot, v16 = _qk_prep(proj, qk_norm[layer, 0], qk_norm[layer, 1], cos, sin, lay,
                                     5 * hg_width, att_heads, kv_heads)
        j = layer // 2
        slab = exp_gate2 if layer % 2 == 0 else exp_up2
        o_att, slab16 = _flash_attention(q_rot, k_rot, v16, lay, kv_heads, BF16, cast_src=slab,
                                         cast_unit=j, cast_units=n_moe)
        expert16[layer % 2][j] = slab16.reshape(n_e, d, -1)
        x = _out_proj(o_hg, o_att, w_out16[layer], x, mod3, lay, layer)
        if layer % 2 == 0:
            x = _dense_ffn(x, mod3, norm_ffn[layer], *dense16, j, lay, layer)
        else:
            h32, idx, gates = _router(x, mod3, norm_ffn[layer], lay, layer, router[j])
            x = _moe(x, h32, idx, gates, expert16[0][j], expert16[1][j], down16[j], n_e, mod3, lay, layer)

    tm = _row_tile(lay, 512)
    n_prompt = x_prompt.shape[0] * x_prompt.shape[1]
    y_prompt = _final_norm(x, final_norm, 0, n_prompt, tm).reshape(x_prompt.shape)
    y_sample = _final_norm(x, final_norm, n_prompt, lay.total - n_prompt, tm).reshape(x_sample.shape)
    return (y_prompt, y_sample)
```

```python
import functools
from typing import NamedTuple

import numpy as np
import jax
import jax.numpy as jnp
from jax import lax
from jax.experimental import pallas as pl
from jax.experimental.pallas import tpu as pltpu

F32 = jnp.float32
BF16 = jnp.bfloat16

EPS = 1e-6
MIN_FORGET = 1e-6
GRID_W = 64
ROPE_THETA = 10000.0
HEAD_DIM = 128
ATT_GROUP = 4
TOP_K = 2
LOG2_E = 1.4426950408889634

V7X_LANES = 128
V7X_MXU_COLS = 256
V7X_VMEM_LIMIT_BYTES = 56 * 1024 * 1024

FLASH_KV_ROWS = 4096
HG_CHUNK = 64
HG_LEAF = 8
HG_BLOCK = 512
HG_HEADS_PER_STEP = 8
HG_UNROLL = 8
FFN_COLS = 1024
MOD_ROWS = 8
PROJ_DTYPE = BF16


class Layout(NamedTuple):
    starts: tuple
    lens: tuple
    total: int


def _layout(x_prompt, x_sample):
    starts, lens, row = [], [], 0
    for arr in (x_prompt, x_sample):
        for _ in range(arr.shape[0]):
            starts.append(row)
            lens.append(arr.shape[1])
            row += arr.shape[1]
    return Layout(tuple(starts), tuple(lens), row)


def _seq_index(row, lay):
    s = jnp.int32(0)
    for st in lay.starts[1:]:
        s = s + (row >= st).astype(jnp.int32)
    return s


def _seq_start(row, lay):
    s = jnp.int32(0)
    for st in lay.starts[1:]:
        s = jnp.where(row >= st, jnp.int32(st), s)
    return s


def _tile(n, target, align):
    best = None
    t = align
    while t <= min(n, target):
        if n % t == 0:
            best = t
        t += align
    assert best is not None, (n, target, align)
    return best


def _row_tile(lay, target):
    return _tile(int(np.gcd.reduce(lay.lens)), target, 16)


def _params(*sem):
    return pltpu.CompilerParams(dimension_semantics=sem, vmem_limit_bytes=V7X_VMEM_LIMIT_BYTES)


def _silu(x):
    return x * (1.0 / (1.0 + jnp.exp(-x)))


def _lower_bounds_body(depth, l_ref, o_ref):
    for d in range(2):
        rows = [l_ref[d * depth + i:d * depth + i + 1, :] for i in range(depth)]
        m = functools.reduce(jnp.maximum, rows)
        e = [jnp.exp(r - m) for r in rows]
        tot = functools.reduce(lambda a, b: a + b, e)
        s = [ei / tot for ei in e]
        run = None
        for i in range(depth):
            run = s[i] if run is None else run + s[i]
            o_ref[d * depth + i:d * depth + i + 1, :] = run - s[0]


def _lower_bounds(lb_logits):
    two, depth, width = lb_logits.shape
    out = pl.pallas_call(
        functools.partial(_lower_bounds_body, depth),
        out_shape=jax.ShapeDtypeStruct((two * depth, width), F32),
        name="hgrn_lower_bounds",
    )(lb_logits.reshape(two * depth, width).astype(F32))
    return out.reshape(two, depth, width)


def _modulation_body(c_ref, w_ref, b_ref, o_ref):
    sc = _silu(c_ref[...]).astype(BF16)
    acc = jnp.dot(sc, w_ref[0].astype(BF16), preferred_element_type=F32)
    o_ref[0] = acc + b_ref[0]


def _modulation(c_rows, w_mod, b_mod):
    depth, d, n = w_mod.shape
    rows = c_rows.shape[0]
    tn = _tile(n, 1024, V7X_LANES)
    return pl.pallas_call(
        _modulation_body,
        out_shape=jax.ShapeDtypeStruct((depth, rows, n), F32),
        grid=(depth, n // tn),
        in_specs=[
            pl.BlockSpec((rows, d), lambda l, j: (0, 0)),
            pl.BlockSpec((1, d, tn), lambda l, j: (l, 0, j)),
            pl.BlockSpec((1, 1, tn), lambda l, j: (l, 0, j)),
        ],
        out_specs=pl.BlockSpec((1, rows, tn), lambda l, j: (l, 0, j)),
        compiler_params=_params("parallel", "parallel"),
        name="adaln_modulation",
    )(c_rows, w_mod, b_mod.reshape(depth, 1, n))


def _mod_spec(lay, tm, layer, part, d, m_axis=0):
    def index(*ids):
        return (layer * MOD_ROWS + _seq_index(ids[m_axis] * tm, lay), 0, part)
    return pl.BlockSpec((1, 1, d), index)


def _norm_modulate(x, gain, scale, shift):
    ms = jnp.mean(x * x, axis=-1, keepdims=True)
    y = x * lax.rsqrt(ms + EPS) * gain
    return y * (1.0 + scale) + shift


def _in_proj_body(x_ref, sh_ref, sc_ref, gain_ref, w_ref, o_ref):
    h = _norm_modulate(x_ref[...], gain_ref[...], sc_ref[0], sh_ref[0])
    o_ref[...] = jnp.dot(h.astype(BF16), w_ref[...], preferred_element_type=F32).astype(o_ref.dtype)


def _in_proj(x, mod3, gain, w, lay, layer, out_dtype):
    t, d = x.shape
    n = w.shape[1]
    tm = _row_tile(lay, 512)
    tn = _tile(n, n // 2 if n % (2 * V7X_MXU_COLS) == 0 else n, V7X_LANES)
    return pl.pallas_call(
        _in_proj_body,
        out_shape=jax.ShapeDtypeStruct((t, n), out_dtype),
        grid=(n // tn, t // tm),
        in_specs=[
            pl.BlockSpec((tm, d), lambda j, m: (m, 0)),
            _mod_spec(lay, tm, layer, 0, d, m_axis=1),
            _mod_spec(lay, tm, layer, 1, d, m_axis=1),
            pl.BlockSpec((1, d), lambda j, m: (0, 0)),
            pl.BlockSpec((d, tn), lambda j, m: (0, j), pipeline_mode=pl.Buffered(1)),
        ],
        out_specs=pl.BlockSpec((tm, tn), lambda j, m: (m, j)),
        compiler_params=_params("parallel", "parallel"),
        name="mixer_in_proj",
    )(x, mod3, mod3, gain.reshape(1, d), w)


def _gates(z, one_minus_lb):
    t = jnp.exp2(jnp.abs(z) * (-LOG2_E))
    r = 1.0 / (1.0 + t)
    k = one_minus_lb * jnp.where(z >= 0, t * r, r)
    log2f = jnp.log2(jnp.maximum(1.0 - k, MIN_FORGET))
    return log2f, k


def _hg_levels(c, leaf):
    widths = []
    w = leaf
    while w < c:
        widths.append(w)
        w *= 2
    return widths


def _hg_wide_constants(c, leaf, reverse):
    t = np.arange(c)[:, None]
    s = np.arange(c)[None, :]
    if reverse:
        t, s = c - 1 - t, c - 1 - s
    cum = (s <= t).astype(np.float32)
    level = np.full((c, c), -1, np.int32)
    level[(s <= t) & (t // leaf == s // leaf)] = 0
    for i, w in enumerate(_hg_levels(c, leaf)):
        level[(s <= t) & (t // (2 * w) == s // (2 * w)) & (t // w != s // w)] = i + 1
    n_levels = 1 + len(_hg_levels(c, leaf))
    wide = np.full((c, n_levels * c), -1, np.int32)
    for l in range(n_levels):
        wide[:, l * c:(l + 1) * c] = np.where(level == l, l, -1)
    return cum, wide, n_levels


def _hg_reference_rows(e_cum, leaf, reverse):
    c, width = e_cum.shape
    mid = leaf // 2 if reverse else leaf // 2 - 1
    refs = [jnp.concatenate(
        [jnp.broadcast_to(e_cum[b + mid:b + mid + 1, :], (leaf, width)) for b in range(0, c, leaf)], axis=0)]
    for w in _hg_levels(c, leaf):
        row = w if reverse else w - 1
        refs.append(jnp.concatenate(
            [jnp.broadcast_to(e_cum[p + row:p + row + 1, :], (2 * w, width)) for p in range(0, c, 2 * w)],
            axis=0))
    return refs


def _hgrn_wide_body(lay, nb, n_levels, n_g, reverse, with_cast, *refs):
    if with_cast:
        *refs, o_ref, w16_ref, state_scr = refs
        *refs, w32_ref = refs
        w16_ref[...] = w32_ref[...].astype(BF16)
    else:
        *refs, o_ref, state_scr = refs
    if reverse:
        q_ref, z_ref, v_ref, g_ref, fwd_ref, lb_ref, gain_ref, cum_ref, wl_ref = refs
    else:
        q_ref, z_ref, v_ref, lb_ref, cum_ref, wl_ref = refs
    step = pl.program_id(1)
    c = HG_CHUNK
    n_chunks = q_ref.shape[0] // c
    width = n_g * HEAD_DIM
    blk = (nb - 1 - step) if reverse else step
    row0 = blk * (n_chunks * c)
    bounds = [st + ln for st, ln in zip(lay.starts, lay.lens)] if reverse else list(lay.starts)
    edge = row0 + n_chunks * c if reverse else row0
    reset = functools.reduce(jnp.logical_or, [edge == b for b in bounds])

    @pl.when(reset)
    def _():
        state_scr[...] = jnp.zeros_like(state_scr)

    nt = (((1,), (1,)), ((), ()))
    tn = (((0,), (0,)), ((), ()))
    last_row = 0 if reverse else c - 1

    def chunk(j, carry):
        jj = (n_chunks - 1 - j) if reverse else j
        rows = pl.ds(pl.multiple_of(jj * c, c), c)
        q = q_ref[rows, :].astype(F32)
        v16 = v_ref[rows, :].astype(BF16)
        states = [state_scr[h] for h in range(n_g)]
        log2f, k = _gates(z_ref[rows, :].astype(F32), 1.0 - lb_ref[...])
        hi = log2f.astype(BF16)
        lo = (log2f - hi.astype(F32)).astype(BF16)
        e2 = jnp.dot(cum_ref[...], jnp.concatenate([hi, lo], axis=1), preferred_element_type=F32)
        e_cum = e2[:, :width] + e2[:, width:]
        e_last = e_cum[last_row:last_row + 1, :]
        refs_l = _hg_reference_rows(e_cum, HG_LEAF, reverse)
        e_leaf = e_cum - refs_l[0]
        q_parts = [q * jnp.exp2(e_leaf)]
        k_parts = [k * jnp.exp2(-e_leaf)]
        for ref in refs_l[1:]:
            delta = lax.bitcast_convert_type(e_cum - ref, jnp.uint32) | jnp.uint32(0x80000000)
            x = jnp.exp2(lax.bitcast_convert_type(delta, F32))
            q_parts.append(q * x)
            k_parts.append(k * x)
        q_in = (q * jnp.exp2(e_cum)).astype(BF16)
        k_out = (k * jnp.exp2(e_last - e_cum)).astype(BF16)
        state_decay = jnp.exp2(e_last)
        wide_level = wl_ref[...]
        lane_blocks = [slice(b, b + V7X_LANES) for b in range(0, n_levels * c, V7X_LANES)]
        masks = [[(wide_level[:, lb_] == l) for l in range(lb_.start // c, lb_.stop // c)] for lb_ in lane_blocks]
        outs = []
        for h in range(n_g):
            cols = slice(h * HEAD_DIM, (h + 1) * HEAD_DIM)
            q_stack = jnp.concatenate([p[:, cols] for p in q_parts], axis=0).astype(BF16)
            k_stack = jnp.concatenate([p[:, cols] for p in k_parts], axis=0).astype(BF16)
            r = lax.dot_general(q_stack, k_stack, nt, preferred_element_type=F32)
            pieces = []
            for lb_, lane_masks in zip(lane_blocks, masks):
                piece = jnp.zeros((c, V7X_LANES), F32)
                for l, mask in zip(range(lb_.start // c, lb_.stop // c), lane_masks):
                    piece = jnp.where(mask, r[l * c:(l + 1) * c, lb_], piece)
                pieces.append(piece)
            scores = jnp.concatenate(pieces, axis=1)
            v_h = v16[:, cols]
            o = jnp.dot(scores.astype(BF16), jnp.concatenate([v_h] * n_levels, axis=0),
                        preferred_element_type=F32)
            o = o + lax.dot_general(q_in[:, cols], states[h].astype(BF16), nt, preferred_element_type=F32)
            upd = lax.dot_general(v_h, k_out[:, cols], tn, preferred_element_type=F32)
            states[h] = states[h] * state_decay[:, cols] + upd
            outs.append(o)
        for h in range(n_g):
            state_scr[h] = states[h]
        if reverse:
            g = g_ref[rows, :].astype(F32)
            fwd = fwd_ref[rows, :]
            for h in range(n_g):
                cols = slice(h * HEAD_DIM, (h + 1) * HEAD_DIM)
                tot = fwd[:, cols] + outs[h]
                ms = jnp.mean(tot * tot, axis=-1, keepdims=True)
                y = tot * lax.rsqrt(ms + EPS) * gain_ref[...]
                o_ref[rows, cols] = (y * _silu(g[:, cols])).astype(o_ref.dtype)
        else:
            for h in range(n_g):
                o_ref[rows, h * HEAD_DIM:(h + 1) * HEAD_DIM] = outs[h]
        return carry

    lax.fori_loop(0, n_chunks, chunk, 0, unroll=HG_UNROLL)


def _hgrn_dir(proj, z_group, lb, fwd_out, out_gain, lay, n_heads, reverse, out_dtype, cast=None):
    t = proj.shape[0]
    lb_rows = _tile(int(np.gcd.reduce(lay.lens)), HG_BLOCK, HG_CHUNK)
    nb = t // lb_rows
    n_g = min(n_heads, HG_HEADS_PER_STEP)
    assert n_heads % n_g == 0
    n_groups = n_heads // n_g
    mats, level, n_levels = _hg_wide_constants(HG_CHUNK, HG_LEAF, reverse)
    width = n_g * HEAD_DIM

    def col(group):
        return lambda g, s: ((nb - 1 - s) if reverse else s, group * n_groups + g)

    blk = (lb_rows, width)
    const = lambda g, s: (0, 0)
    in_specs = [pl.BlockSpec(blk, col(0)), pl.BlockSpec(blk, col(z_group)), pl.BlockSpec(blk, col(3))]
    args = [proj, proj, proj]
    if reverse:
        in_specs += [pl.BlockSpec(blk, col(4)), pl.BlockSpec(blk, col(0))]
        args += [proj, fwd_out]
    in_specs.append(pl.BlockSpec((1, width), lambda g, s: (0, g)))
    args.append(lb.reshape(1, n_heads * HEAD_DIM))
    if reverse:
        in_specs.append(pl.BlockSpec((1, HEAD_DIM), const))
        args.append(out_gain.reshape(1, HEAD_DIM))
    in_specs += [pl.BlockSpec(mats.shape, const), pl.BlockSpec(level.shape, const)]
    args += [jnp.asarray(mats, BF16), jnp.asarray(level)]
    out_specs = pl.BlockSpec(blk, col(0))
    out_shape = jax.ShapeDtypeStruct((t, n_heads * HEAD_DIM), out_dtype)
    if cast is not None:
        src, first_row, n_rows, col_block, n_cols = cast
        assert n_groups == 1 and n_rows % (16 * nb) == 0 and first_row % (n_rows // nb) == 0
        chunk = n_rows // nb
        in_specs.append(pl.BlockSpec((chunk, n_cols), lambda g, s: (first_row // chunk + s, col_block)))
        args.append(src)
        out_specs = (out_specs, pl.BlockSpec((chunk, n_cols), lambda g, s: (s, 0)))
        out_shape = (out_shape, jax.ShapeDtypeStruct((n_rows, n_cols), BF16))
    return pl.pallas_call(
        functools.partial(_hgrn_wide_body, lay, nb, n_levels, n_g, reverse, cast is not None),
        out_shape=out_shape,
        grid=(n_groups, nb),
        in_specs=in_specs,
        out_specs=out_specs,
        scratch_shapes=[pltpu.VMEM((n_g, HEAD_DIM, HEAD_DIM), F32)],
        compiler_params=_params("parallel", "arbitrary"),
        name="hgrn2_bwd" if reverse else "hgrn2_fwd",
    )(*args)


def _hgrn(proj, lbs_f, lbs_b, out_gain, lay, n_heads, out_dtype, cast_fwd=None, cast_bwd=None):
    fwd = _hgrn_dir(proj, 1, lbs_f, None, None, lay, n_heads, False, F32, cast_fwd)
    o_fwd, w_a = fwd if cast_fwd is not None else (fwd, None)
    bwd = _hgrn_dir(proj, 2, lbs_b, o_fwd, out_gain, lay, n_heads, True, out_dtype, cast_bwd)
    o, w_b = bwd if cast_bwd is not None else (bwd, None)
    return o, w_a, w_b


def _rope_tables(n_pos):
    n_freq = HEAD_DIM // 4
    pos = jnp.arange(n_pos)
    coords = jnp.stack([pos // GRID_W, pos % GRID_W], axis=-1).astype(F32)
    inv = ROPE_THETA ** (-jnp.arange(n_freq, dtype=F32) / n_freq)
    ang = coords[:, :, None] * inv
    cos = jnp.broadcast_to(jnp.cos(ang)[:, :, None, :], (n_pos, 2, 2, n_freq))
    sin = jnp.sin(ang)
    sin = jnp.stack([-sin, sin], axis=2)
    return cos.reshape(n_pos, HEAD_DIM), sin.reshape(n_pos, HEAD_DIM)


def _qk_prep_body(n_q, n_kv, q_ref, k_ref, v_ref, cos_ref, sin_ref, qg_ref, kg_ref, swap_ref,
                  qo_ref, ko_ref, vo_ref):
    cos = cos_ref[...]
    sin = sin_ref[...]
    swap = swap_ref[...]

    def norm_rope(x, gain, scale):
        ms = jnp.mean(x * x, axis=-1, keepdims=True)
        y = x * lax.rsqrt(ms + EPS) * gain
        partner = jnp.dot(y.astype(BF16), swap, preferred_element_type=F32)
        out = y * cos + partner * sin
        return out * scale if scale is not None else out

    for h in range(n_q):
        cols = slice(h * HEAD_DIM, (h + 1) * HEAD_DIM)
        qo_ref[:, cols] = norm_rope(q_ref[:, cols].astype(F32), qg_ref[...],
                                    HEAD_DIM ** -0.5 * LOG2_E).astype(qo_ref.dtype)
    for h in range(n_kv):
        cols = slice(h * HEAD_DIM, (h + 1) * HEAD_DIM)
        ko_ref[:, cols] = norm_rope(k_ref[:, cols].astype(F32), kg_ref[...], None).astype(ko_ref.dtype)
    vo_ref[...] = v_ref[...].astype(vo_ref.dtype)


def _qk_prep(proj, q_gain, k_gain, cos, sin, lay, q_col, n_q, n_kv):
    t = proj.shape[0]
    tm = _row_tile(lay, 256)
    qw, kw = n_q * HEAD_DIM, n_kv * HEAD_DIM
    assert q_col % qw == 0 and (q_col + qw) % kw == 0

    def pos_block(m):
        row = m * tm
        return ((row - _seq_start(row, lay)) // tm, 0)

    quarter = HEAD_DIM // 4
    lane = np.arange(HEAD_DIM)
    source = np.where(lane % (2 * quarter) < quarter, lane + quarter, lane - quarter)
    swap = (lane[:, None] == source[None, :]).astype(np.float32)

    return pl.pallas_call(
        functools.partial(_qk_prep_body, n_q, n_kv),
        out_shape=(jax.ShapeDtypeStruct((t, qw), BF16), jax.ShapeDtypeStruct((t, kw), BF16),
                   jax.ShapeDtypeStruct((t, kw), BF16)),
        grid=(t // tm,),
        in_specs=[
            pl.BlockSpec((tm, qw), lambda m: (m, q_col // qw)),
            pl.BlockSpec((tm, kw), lambda m: (m, (q_col + qw) // kw)),
            pl.BlockSpec((tm, kw), lambda m: (m, (q_col + qw) // kw + 1)),
            pl.BlockSpec((tm, HEAD_DIM), pos_block),
            pl.BlockSpec((tm, HEAD_DIM), pos_block),
            pl.BlockSpec((1, HEAD_DIM), lambda m: (0, 0)),
            pl.BlockSpec((1, HEAD_DIM), lambda m: (0, 0)),
            pl.BlockSpec((HEAD_DIM, HEAD_DIM), lambda m: (0, 0)),
        ],
        out_specs=(pl.BlockSpec((tm, qw), lambda m: (m, 0)), pl.BlockSpec((tm, kw), lambda m: (m, 0)),
                   pl.BlockSpec((tm, kw), lambda m: (m, 0))),
        compiler_params=_params("parallel"),
        name="attn_qk_prep",
    )(proj, proj, proj, cos, sin, q_gain.reshape(1, HEAD_DIM), k_gain.reshape(1, HEAD_DIM),
      jnp.asarray(swap, BF16))


def _flash_body(with_cast, qb_ref, kb_ref, flag_ref, q_ref, k_ref, v_ref, *rest):
    if with_cast:
        w32_ref, o_ref, w16_ref, m_scr, l_scr, acc_scr = rest
        w16_ref[...] = w32_ref[...].astype(BF16)
    else:
        o_ref, m_scr, l_scr, acc_scr = rest
    i = pl.program_id(1)
    flags = flag_ref[i]
    tq = q_ref.shape[0]

    @pl.when((flags & 1) != 0)
    def _():
        m_scr[...] = jnp.full_like(m_scr, -jnp.inf)
        l_scr[...] = jnp.zeros_like(l_scr)
        acc_scr[...] = jnp.zeros_like(acc_scr)

    q = q_ref[...]
    qs = jnp.concatenate([q[:, g * HEAD_DIM:(g + 1) * HEAD_DIM] for g in range(ATT_GROUP)], axis=0)
    s = lax.dot_general(qs, k_ref[...], (((1,), (1,)), ((), ())), preferred_element_type=F32)
    m_prev = m_scr[...]
    m_new = jnp.maximum(m_prev, jnp.max(s, axis=-1, keepdims=True))
    alpha = jnp.exp2(m_prev - m_new)
    p = jnp.exp2(s - m_new)
    l_scr[...] = alpha * l_scr[...] + jnp.sum(p, axis=-1, keepdims=True)
    acc_scr[...] = alpha * acc_scr[...] + jnp.dot(p.astype(BF16), v_ref[...], preferred_element_type=F32)
    m_scr[...] = m_new

    @pl.when((flags & 2) != 0)
    def _():
        out = acc_scr[...] / l_scr[...]
        for g in range(ATT_GROUP):
            o_ref[:, g * HEAD_DIM:(g + 1) * HEAD_DIM] = out[g * tq:(g + 1) * tq].astype(o_ref.dtype)


def _flash_attention(q_rot, k_rot, v16, lay, n_kv, out_dtype, cast_src=None, cast_unit=0, cast_units=1):
    t = q_rot.shape[0]
    g = int(np.gcd.reduce(lay.lens))
    tq = _tile(g, 256, 16)
    tk = _tile(g, FLASH_KV_ROWS, V7X_LANES)
    qb, kb, flags = [], [], []
    for st, ln in zip(lay.starts, lay.lens):
        for qi in range(ln // tq):
            for ki in range(ln // tk):
                qb.append(st // tq + qi)
                kb.append(st // tk + ki)
                flags.append((1 if ki == 0 else 0) | (2 if ki == ln // tk - 1 else 0))
    n_items = len(qb)
    gw = ATT_GROUP * HEAD_DIM
    in_specs = [
        pl.BlockSpec((tq, gw), lambda h, i, qb, kb, fl: (qb[i], h)),
        pl.BlockSpec((tk, HEAD_DIM), lambda h, i, qb, kb, fl: (kb[i], h)),
        pl.BlockSpec((tk, HEAD_DIM), lambda h, i, qb, kb, fl: (kb[i], h)),
    ]
    out_specs = pl.BlockSpec((tq, gw), lambda h, i, qb, kb, fl: (qb[i], h))
    out_shape = jax.ShapeDtypeStruct((t, n_kv * gw), out_dtype)
    args = [q_rot, k_rot, v16]
    if cast_src is not None:
        rows, cols = cast_src.shape[0] // cast_units, cast_src.shape[1]
        n_steps = n_kv * n_items
        chunk = min(c for c in range(16, rows + 1, 16) if rows % c == 0 and rows // c <= n_steps)
        n_chunks = rows // chunk

        def chunk_of(h, i):
            return jnp.minimum(h * n_items + i, n_chunks - 1)

        in_specs.append(pl.BlockSpec((chunk, cols),
                                     lambda h, i, qb, kb, fl: (cast_unit * n_chunks + chunk_of(h, i), 0)))
        out_specs = (out_specs, pl.BlockSpec((chunk, cols), lambda h, i, qb, kb, fl: (chunk_of(h, i), 0)))
        out_shape = (out_shape, jax.ShapeDtypeStruct((rows, cols), BF16))
        args.append(cast_src)
    grid_spec = pltpu.PrefetchScalarGridSpec(
        num_scalar_prefetch=3,
        grid=(n_kv, n_items),
        in_specs=in_specs,
        out_specs=out_specs,
        scratch_shapes=[pltpu.VMEM((ATT_GROUP * tq, 1), F32), pltpu.VMEM((ATT_GROUP * tq, 1), F32),
                        pltpu.VMEM((ATT_GROUP * tq, HEAD_DIM), F32)],
    )
    return pl.pallas_call(
        functools.partial(_flash_body, cast_src is not None),
        out_shape=out_shape,
        grid_spec=grid_spec,
        compiler_params=_params("arbitrary", "arbitrary"),
        name="attn_flash",
    )(jnp.asarray(qb, jnp.int32), jnp.asarray(kb, jnp.int32), jnp.asarray(flags, jnp.int32), *args)


def _out_proj_body(a_ref, b_ref, wa_ref, wb_ref, x_ref, g_ref, o_ref):
    acc = jnp.dot(a_ref[...], wa_ref[...], preferred_element_type=F32)
    acc = acc + jnp.dot(b_ref[...], wb_ref[...], preferred_element_type=F32)
    o_ref[...] = x_ref[...] + g_ref[0] * acc


def _out_proj(o_hg, o_att, w_out, x, mod3, lay, layer):
    t, d = x.shape
    ka, kb = o_hg.shape[1], o_att.shape[1]
    tm = _row_tile(lay, 512)
    return pl.pallas_call(
        _out_proj_body,
        out_shape=jax.ShapeDtypeStruct((t, d), F32),
        grid=(t // tm,),
        in_specs=[
            pl.BlockSpec((tm, ka), lambda m: (m, 0)),
            pl.BlockSpec((tm, kb), lambda m: (m, 0)),
            pl.BlockSpec((ka, d), lambda m: (0, 0)),
            pl.BlockSpec((kb, d), lambda m: (1, 0)),
            pl.BlockSpec((tm, d), lambda m: (m, 0)),
            _mod_spec(lay, tm, layer, 2, d),
        ],
        out_specs=pl.BlockSpec((tm, d), lambda m: (m, 0)),
        compiler_params=_params("parallel"),
        name="mixer_out_proj",
    )(o_hg, o_att, w_out, w_out, x, mod3)


def _router_body(x_ref, sh_ref, sc_ref, gain_ref, wr_ref, o_ref, idx_ref, gate_ref):
    h = _norm_modulate(x_ref[...], gain_ref[...], sc_ref[0], sh_ref[0])
    o_ref[...] = h.astype(o_ref.dtype)
    wr = wr_ref[...]
    h_hi, w_hi = h.astype(BF16), wr.astype(BF16)
    h_lo = (h - h_hi.astype(F32)).astype(BF16)
    w_lo = (wr - w_hi.astype(F32)).astype(BF16)
    logits = (jnp.dot(h_hi, w_hi, preferred_element_type=F32) + jnp.dot(h_lo, w_hi, preferred_element_type=F32)
              + jnp.dot(h_hi, w_lo, preferred_element_type=F32))
    n_e = logits.shape[1]
    lane = lax.broadcasted_iota(jnp.int32, logits.shape, 1)
    m1 = jnp.max(logits, axis=-1, keepdims=True)
    i1 = jnp.min(jnp.where(logits == m1, lane, n_e), axis=-1, keepdims=True)
    rest = jnp.where(lane == i1, -jnp.inf, logits)
    m2 = jnp.max(rest, axis=-1, keepdims=True)
    i2 = jnp.min(jnp.where(rest == m2, lane, n_e), axis=-1, keepdims=True)
    e = jnp.exp(m2 - m1)
    tot = 1.0 + e
    two = lax.broadcasted_iota(jnp.int32, idx_ref.shape, 1)
    idx_ref[...] = jnp.where(two == 0, i1, i2)
    gate_ref[...] = jnp.where(two == 0, 1.0 / tot, e / tot)


def _router(x, mod3, gain, lay, layer, w_router):
    t, d = x.shape
    tm = _row_tile(lay, 256)
    in_specs = [
        pl.BlockSpec((tm, d), lambda m: (m, 0)),
        _mod_spec(lay, tm, layer, 3, d),
        _mod_spec(lay, tm, layer, 4, d),
        pl.BlockSpec((1, d), lambda m: (0, 0)),
    ]
    row_spec = pl.BlockSpec((tm, d), lambda m: (m, 0))
    n_e = w_router.shape[1]
    small = pl.BlockSpec((tm, TOP_K), lambda m: (m, 0))
    return pl.pallas_call(
        _router_body,
        out_shape=(jax.ShapeDtypeStruct((t, d), F32), jax.ShapeDtypeStruct((t, TOP_K), jnp.int32),
                   jax.ShapeDtypeStruct((t, TOP_K), F32)),
        grid=(t // tm,),
        in_specs=in_specs + [pl.BlockSpec((d, n_e), lambda m: (0, 0))],
        out_specs=(row_spec, small, small),
        compiler_params=_params("parallel"), name="moe_router",
    )(x, mod3, mod3, gain.reshape(1, d), w_router)


def _swiglu_step(x16, wg_ref, wu_ref, wd_refs, acc_scr):
    gate = jnp.dot(x16, wg_ref[0], preferred_element_type=F32)
    up = jnp.dot(x16, wu_ref[0], preferred_element_type=F32)
    act = (_silu(gate) * up).astype(BF16)
    col = 0
    for wd_ref in wd_refs:
        n = wd_ref.shape[2]
        acc_scr[:, col:col + n] += jnp.dot(act, wd_ref[0], preferred_element_type=F32)
        col += n


def _dense_ffn_body(x_ref, sh_ref, sc_ref, gain_ref, wg_ref, wu_ref, wd_ref, g_ref, o_ref, x16_scr, acc_scr):
    f = pl.program_id(1)

    @pl.when(f == 0)
    def _():
        x16_scr[...] = _norm_modulate(x_ref[...], gain_ref[...], sc_ref[0], sh_ref[0]).astype(BF16)
        acc_scr[...] = jnp.zeros_like(acc_scr)

    _swiglu_step(x16_scr[...], wg_ref, wu_ref, (wd_ref,), acc_scr)

    @pl.when(f == pl.num_programs(1) - 1)
    def _():
        o_ref[...] = x_ref[...] + g_ref[0] * acc_scr[...]


def _dense_ffn(x, mod3, gain, w_gate, w_up, w_down, j, lay, layer):
    t, d = x.shape
    fdim = w_gate.shape[2]
    tm = _row_tile(lay, 512)
    tf = _tile(fdim, FFN_COLS, V7X_LANES)
    return pl.pallas_call(
        _dense_ffn_body,
        out_shape=jax.ShapeDtypeStruct((t, d), F32),
        grid=(t // tm, fdim // tf),
        in_specs=[
            pl.BlockSpec((tm, d), lambda m, f: (m, 0)),
            _mod_spec(lay, tm, layer, 3, d),
            _mod_spec(lay, tm, layer, 4, d),
            pl.BlockSpec((1, d), lambda m, f: (0, 0)),
            pl.BlockSpec((1, d, tf), lambda m, f: (j, 0, f)),
            pl.BlockSpec((1, d, tf), lambda m, f: (j, 0, f)),
            pl.BlockSpec((1, tf, d), lambda m, f: (j, f, 0)),
            _mod_spec(lay, tm, layer, 5, d),
        ],
        out_specs=pl.BlockSpec((tm, d), lambda m, f: (m, 0)),
        scratch_shapes=[pltpu.VMEM((tm, d), BF16), pltpu.VMEM((tm, d), F32)],
        compiler_params=_params("parallel", "arbitrary"),
        name="swiglu_residual",
    )(x, mod3, mod3, gain.reshape(1, d), w_gate, w_up, w_down, mod3)


def _expert_ffn_body(n_f, n_down, be_ref, nu_ref, tok_ref, h_hbm, wg_ref, wu_ref, *rest):
    wd_refs = rest[:n_down]
    o_ref, xbuf, x16_scr, acc_scr, sems = rest[n_down:]
    m = pl.program_id(0)
    f = pl.program_id(1)
    tm = x16_scr.shape[0]
    n_used = nu_ref[0]
    used = m < n_used
    last = f == n_f - 1

    def row_copy(src_row, slot, dst_row):
        return pltpu.make_async_copy(h_hbm.at[pl.ds(src_row, 1)], xbuf.at[slot, pl.ds(dst_row, 1)],
                                     sems.at[slot])

    slot = m % 2
    share = tm // n_f
    extra = tm - share * n_f

    @pl.when(jnp.logical_and(used, f == 0))
    def _():
        @pl.when(m == 0)
        def _():
            def issue(r, carry):
                row_copy(tok_ref[r], 0, r).start()
                return carry
            lax.fori_loop(0, tm, issue, 0)

        for _ in range(tm):
            row_copy(0, slot, 0).wait()
        x16_scr[...] = xbuf[slot].astype(BF16)
        acc_scr[...] = jnp.zeros_like(acc_scr)

    first = (m + 1) * tm

    @pl.when(used)
    def _():
        for r in range(share):
            row = f * share + r
            row_copy(tok_ref[first + row], 1 - slot, row).start()
        _swiglu_step(x16_scr[...], wg_ref, wu_ref, wd_refs, acc_scr)

    @pl.when(jnp.logical_and(used, last))
    def _():
        for r in range(share * n_f, share * n_f + extra):
            row_copy(tok_ref[first + r], 1 - slot, r).start()
        o_ref[...] = acc_scr[...]

        @pl.when(m + 1 == n_used)
        def _():
            for _ in range(tm):
                row_copy(0, 1 - slot, 0).wait()

    @pl.when(jnp.logical_and(jnp.logical_not(used), last))
    def _():
        o_ref[...] = jnp.zeros_like(o_ref)


def _expert_ffn(h32, row_token, w_gate, w_up, w_down_slabs, block_expert, n_used, tm):
    d = h32.shape[1]
    r = row_token.shape[0]
    fdim = w_gate.shape[2]
    tf = _tile(fdim, FFN_COLS, V7X_LANES)
    nf = fdim // tf

    def fcol(m, f, nu):
        return jnp.where(m < nu[0], f, nf - 1)

    grid_spec = pltpu.PrefetchScalarGridSpec(
        num_scalar_prefetch=3,
        grid=(r // tm, nf),
        in_specs=[
            pl.BlockSpec(memory_space=pl.ANY),
            pl.BlockSpec((1, d, tf), lambda m, f, be, nu, tok: (be[m], 0, fcol(m, f, nu))),
            pl.BlockSpec((1, d, tf), lambda m, f, be, nu, tok: (be[m], 0, fcol(m, f, nu))),
        ] + [pl.BlockSpec((1, tf, w.shape[2]), lambda m, f, be, nu, tok: (be[m], fcol(m, f, nu), 0))
             for w in w_down_slabs],
        out_specs=pl.BlockSpec((tm, d), lambda m, f, be, nu, tok: (m, 0)),
        scratch_shapes=[pltpu.VMEM((2, tm, d), F32), pltpu.VMEM((tm, d), BF16), pltpu.VMEM((tm, d), F32),
                        pltpu.SemaphoreType.DMA((2,))],
    )
    return pl.pallas_call(
        functools.partial(_expert_ffn_body, nf, len(w_down_slabs)),
        out_shape=jax.ShapeDtypeStruct((r, d), F32),
        grid_spec=grid_spec,
        compiler_params=_params("arbitrary", "arbitrary"),
        name="swiglu_experts",
    )(block_expert, n_used, row_token, h32, w_gate, w_up, *w_down_slabs)


def _combine_body(dest_ref, yb_hbm, x_ref, gate_ref, g_ref, o_ref, buf, sems):
    tm = x_ref.shape[0]
    i = pl.program_id(0)
    slot = i % 2

    def row_copy(src_row, slot_, k, dst_row):
        return pltpu.make_async_copy(yb_hbm.at[pl.ds(src_row, 1)], buf.at[slot_, k, pl.ds(dst_row, 1)],
                                     sems.at[slot_])

    def gather(tile, slot_):
        def issue(r, carry):
            for k in range(TOP_K):
                row_copy(dest_ref[(tile * tm + r) * TOP_K + k], slot_, k, r).start()
            return carry
        lax.fori_loop(0, tm, issue, 0)

    @pl.when(i == 0)
    def _():
        gather(0, 0)

    @pl.when(i + 1 < pl.num_programs(0))
    def _():
        gather(i + 1, 1 - slot)

    for _ in range(tm * TOP_K):
        row_copy(0, slot, 0, 0).wait()
    gates = gate_ref[...]
    y = buf[slot, 0] * gates[:, 0:1]
    for k in range(1, TOP_K):
        y = y + buf[slot, k] * gates[:, k:k + 1]
    o_ref[...] = x_ref[...] + g_ref[0] * y


def _combine(yb, dest, gates, x, mod3, lay, layer):
    t, d = x.shape
    tm = _row_tile(lay, 256)
    grid_spec = pltpu.PrefetchScalarGridSpec(
        num_scalar_prefetch=1,
        grid=(t // tm,),
        in_specs=[
            pl.BlockSpec(memory_space=pl.ANY),
            pl.BlockSpec((tm, d), lambda m, dest: (m, 0)),
            pl.BlockSpec((tm, TOP_K), lambda m, dest: (m, 0)),
            _mod_spec(lay, tm, layer, 5, d),
        ],
        out_specs=pl.BlockSpec((tm, d), lambda m, dest: (m, 0)),
        scratch_shapes=[pltpu.VMEM((2, TOP_K, tm, d), yb.dtype), pltpu.SemaphoreType.DMA((2,))],
    )
    return pl.pallas_call(
        _combine_body,
        out_shape=jax.ShapeDtypeStruct((t, d), F32),
        grid_spec=grid_spec,
        compiler_params=_params("arbitrary"),
        name="moe_combine",
    )(dest, yb, x, gates, mod3)


def _moe(x, h32, idx, gates, w_gate, w_up, w_down_slabs, n_e, mod3, lay, layer):
    t, d = x.shape
    tm = _row_tile(lay, 512)
    n_slots = t * TOP_K
    cap = n_slots + n_e * tm
    e = idx.reshape(-1)
    onehot = (e[:, None] == jnp.arange(n_e, dtype=jnp.int32)[None, :]).astype(jnp.int32)
    csum = jnp.cumsum(onehot, axis=0)
    rank = jnp.take_along_axis(csum, e[:, None], axis=1)[:, 0] - 1
    counts = csum[-1]
    padded = (counts + tm - 1) // tm * tm
    pad_end = jnp.cumsum(padded)
    pad_start = pad_end - padded
    dest = (pad_start[e] + rank).astype(jnp.int32)
    row_token = jnp.zeros((cap,), jnp.int32).at[dest].set(jnp.arange(n_slots, dtype=jnp.int32) // TOP_K)
    block_first = jnp.arange(cap // tm, dtype=jnp.int32) * tm
    block_e = jnp.minimum(jnp.searchsorted(pad_end, block_first, side='right'), n_e - 1).astype(jnp.int32)
    n_used = (pad_end[-1:] // tm).astype(jnp.int32)
    block_e = jnp.where(block_first < pad_end[-1], block_e, block_e[jnp.maximum(n_used[0] - 1, 0)])
    yb = _expert_ffn(h32, row_token, w_gate, w_up, w_down_slabs, block_e, n_used, tm)
    return _combine(yb, dest, gates, x, mod3, lay, layer)


def _final_norm_body(x_ref, gain_ref, o_ref):
    x = x_ref[...]
    ms = jnp.mean(x * x, axis=-1, keepdims=True)
    o_ref[...] = x * lax.rsqrt(ms + EPS) * gain_ref[...]


def _final_norm(x, gain, row0, n_rows, tm):
    d = x.shape[1]
    first = row0 // tm
    return pl.pallas_call(
        _final_norm_body,
        out_shape=jax.ShapeDtypeStruct((n_rows, d), F32),
        grid=(n_rows // tm,),
        in_specs=[pl.BlockSpec((tm, d), lambda m: (first + m, 0)), pl.BlockSpec((1, d), lambda m: (0, 0))],
        out_specs=pl.BlockSpec((tm, d), lambda m: (m, 0)),
        compiler_params=_params("parallel"),
        name="final_norm",
    )(x, gain.reshape(1, d))


def kernel(x_prompt, x_sample, c_prompt, c_sample, w_mod, b_mod, norm_mix, norm_ffn, w_in, hg_lb_logits,
           hg_out_norm, qk_norm, w_out, ffn_gate, ffn_up, ffn_down, router, exp_gate, exp_up, exp_down,
           final_norm):
    depth, d = norm_mix.shape
    lay = _layout(x_prompt, x_sample)
    n_seq = len(lay.starts)
    assert n_seq <= MOD_ROWS
    hg_heads = hg_lb_logits.shape[2] // HEAD_DIM
    kv_heads = (w_in.shape[2] - 5 * hg_heads * HEAD_DIM) // HEAD_DIM // (ATT_GROUP + 2)
    att_heads = kv_heads * ATT_GROUP
    hg_width = hg_heads * HEAD_DIM
    n_e = exp_gate.shape[1]

    x = jnp.concatenate([x_prompt.reshape(-1, d), x_sample.reshape(-1, d)], axis=0)
    c_rows = jnp.concatenate([c_prompt, c_sample, jnp.zeros((MOD_ROWS - n_seq, d), F32)], axis=0)
    mod3 = _modulation(c_rows, w_mod, b_mod).reshape(depth * MOD_ROWS, 1, 6 * d)
    lbs = _lower_bounds(hg_lb_logits)
    cos, sin = _rope_tables(max(lay.lens))

    w_in16 = w_in.astype(BF16)
    w_out16 = w_out.astype(BF16)
    dense16 = [w.astype(BF16) for w in (ffn_gate, ffn_up, ffn_down)]
    n_moe = exp_gate.shape[0]
    assert depth == 2 * n_moe, "one attention call per (MoE layer, gate / up) weight slab"
    exp_gate2 = exp_gate.reshape(-1, exp_gate.shape[-1])
    exp_up2 = exp_up.reshape(-1, exp_up.shape[-1])
    exp_down2 = exp_down.reshape(-1, d)
    down_rows = n_e * exp_down.shape[2]
    expert16 = [[None] * n_moe, [None] * n_moe]
    down16 = [None] * n_moe

    for layer in range(depth):
        proj = _in_proj(x, mod3, norm_mix[layer], w_in16[layer], lay, layer, PROJ_DTYPE)
        if layer % 2 == 0:
            casts = [(exp_down2, (layer // 2) * down_rows, down_rows, half, d // 2) for half in range(2)]
            o_hg, *slabs = _hgrn(proj, lbs[0, layer], lbs[1, layer], hg_out_norm[layer], lay, hg_heads, BF16,
                                 *casts)
            down16[layer // 2] = [s.reshape(n_e, -1, d // 2) for s in slabs]
        else:
            o_hg, _, _ = _hgrn(proj, lbs[0, layer], lbs[1, layer], hg_out_norm[layer], lay, hg_heads, BF16)
        q_rot, k_rot, v16 = _qk_prep(proj, qk_norm[layer, 0], qk_norm[layer, 1], cos, sin, lay,
                                     5 * hg_width, att_heads, kv_heads)
        j = layer // 2
        slab = exp_gate2 if layer % 2 == 0 else exp_up2
        o_att, slab16 = _flash_attention(q_rot, k_rot, v16, lay, kv_heads, BF16, cast_src=slab,
                                         cast_unit=j, cast_units=n_moe)
        expert16[layer % 2][j] = slab16.reshape(n_e, d, -1)
        x = _out_proj(o_hg, o_att, w_out16[layer], x, mod3, lay, layer)
        if layer % 2 == 0:
            x = _dense_ffn(x, mod3, norm_ffn[layer], *dense16, j, lay, layer)
        else:
            h32, idx, gates = _router(x, mod3, norm_ffn[layer], lay, layer, router[j])
            x = _moe(x, h32, idx, gates, expert16[0][j], expert16[1][j], down16[j], n_e, mod3, lay, layer)

    tm = _row_tile(lay, 512)
    n_prompt = x_prompt.shape[0] * x_prompt.shape[1]
    y_prompt = _final_norm(x, final_norm, 0, n_prompt, tm).reshape(x_prompt.shape)
    y_sample = _final_norm(x, final_norm, n_prompt, lay.total - n_prompt, tm).reshape(x_sample.shape)
    return (y_prompt, y_sample)
```

```python
import functools
from typing import NamedTuple

import numpy as np
import jax
import jax.numpy as jnp
from jax import lax
from jax.experimental import pallas as pl
from jax.experimental.pallas import tpu as pltpu

F32 = jnp.float32
BF16 = jnp.bfloat16

EPS = 1e-6
MIN_FORGET = 1e-6
GRID_W = 64
ROPE_THETA = 10000.0
HEAD_DIM = 128
ATT_GROUP = 4
TOP_K = 2
LOG2_E = 1.4426950408889634

V7X_LANES = 128
V7X_MXU_COLS = 256
V7X_VMEM_LIMIT_BYTES = 56 * 1024 * 1024

FLASH_KV_ROWS = 4096
FLASH_Q_SUBBLOCKS = 2
HG_CHUNK = 64
HG_LEAF = 8
HG_BLOCK = 512
HG_HEADS_PER_STEP = 8
HG_UNROLL = 8
FFN_COLS = 1024
MOD_ROWS = 8
PROJ_DTYPE = BF16


class Layout(NamedTuple):
    starts: tuple
    lens: tuple
    total: int


def _layout(x_prompt, x_sample):
    starts, lens, row = [], [], 0
    for arr in (x_prompt, x_sample):
        for _ in range(arr.shape[0]):
            starts.append(row)
            lens.append(arr.shape[1])
            row += arr.shape[1]
    return Layout(tuple(starts), tuple(lens), row)


def _seq_index(row, lay):
    s = jnp.int32(0)
    for st in lay.starts[1:]:
        s = s + (row >= st).astype(jnp.int32)
    return s


def _seq_start(row, lay):
    s = jnp.int32(0)
    for st in lay.starts[1:]:
        s = jnp.where(row >= st, jnp.int32(st), s)
    return s


def _tile(n, target, align):
    best = None
    t = align
    while t <= min(n, target):
        if n % t == 0:
            best = t
        t += align
    assert best is not None, (n, target, align)
    return best


def _row_tile(lay, target):
    return _tile(int(np.gcd.reduce(lay.lens)), target, 16)


def _params(*sem):
    return pltpu.CompilerParams(dimension_semantics=sem, vmem_limit_bytes=V7X_VMEM_LIMIT_BYTES)


def _silu(x):
    return x * (1.0 / (1.0 + jnp.exp(-x)))


def _lower_bounds_body(depth, l_ref, o_ref):
    for d in range(2):
        rows = [l_ref[d * depth + i:d * depth + i + 1, :] for i in range(depth)]
        m = functools.reduce(jnp.maximum, rows)
        e = [jnp.exp(r - m) for r in rows]
        tot = functools.reduce(lambda a, b: a + b, e)
        s = [ei / tot for ei in e]
        run = None
        for i in range(depth):
            run = s[i] if run is None else run + s[i]
            o_ref[d * depth + i:d * depth + i + 1, :] = run - s[0]


def _lower_bounds(lb_logits):
    two, depth, width = lb_logits.shape
    out = pl.pallas_call(
        functools.partial(_lower_bounds_body, depth),
        out_shape=jax.ShapeDtypeStruct((two * depth, width), F32),
        name="hgrn_lower_bounds",
    )(lb_logits.reshape(two * depth, width).astype(F32))
    return out.reshape(two, depth, width)


def _modulation_body(c_ref, w_ref, b_ref, o_ref):
    sc = _silu(c_ref[...]).astype(BF16)
    acc = jnp.dot(sc, w_ref[0].astype(BF16), preferred_element_type=F32)
    o_ref[0] = acc + b_ref[0]


def _modulation(c_rows, w_mod, b_mod):
    depth, d, n = w_mod.shape
    rows = c_rows.shape[0]
    tn = _tile(n, 1024, V7X_LANES)
    return pl.pallas_call(
        _modulation_body,
        out_shape=jax.ShapeDtypeStruct((depth, rows, n), F32),
        grid=(depth, n // tn),
        in_specs=[
            pl.BlockSpec((rows, d), lambda l, j: (0, 0)),
            pl.BlockSpec((1, d, tn), lambda l, j: (l, 0, j)),
            pl.BlockSpec((1, 1, tn), lambda l, j: (l, 0, j)),
        ],
        out_specs=pl.BlockSpec((1, rows, tn), lambda l, j: (l, 0, j)),
        compiler_params=_params("parallel", "parallel"),
        name="adaln_modulation",
    )(c_rows, w_mod, b_mod.reshape(depth, 1, n))


def _mod_spec(lay, tm, layer, part, d, m_axis=0):
    def index(*ids):
        return (layer * MOD_ROWS + _seq_index(ids[m_axis] * tm, lay), 0, part)
    return pl.BlockSpec((1, 1, d), index)


def _norm_modulate(x, gain, scale, shift):
    ms = jnp.mean(x * x, axis=-1, keepdims=True)
    y = x * lax.rsqrt(ms + EPS) * gain
    return y * (1.0 + scale) + shift


def _in_proj_body(x_ref, sh_ref, sc_ref, gain_ref, w_ref, o_ref):
    h = _norm_modulate(x_ref[...], gain_ref[...], sc_ref[0], sh_ref[0])
    o_ref[...] = jnp.dot(h.astype(BF16), w_ref[...], preferred_element_type=F32).astype(o_ref.dtype)


def _in_proj(x, mod3, gain, w, lay, layer, out_dtype):
    t, d = x.shape
    n = w.shape[1]
    tm = _row_tile(lay, 512)
    tn = _tile(n, n // 2 if n % (2 * V7X_MXU_COLS) == 0 else n, V7X_LANES)
    return pl.pallas_call(
        _in_proj_body,
        out_shape=jax.ShapeDtypeStruct((t, n), out_dtype),
        grid=(n // tn, t // tm),
        in_specs=[
            pl.BlockSpec((tm, d), lambda j, m: (m, 0)),
            _mod_spec(lay, tm, layer, 0, d, m_axis=1),
            _mod_spec(lay, tm, layer, 1, d, m_axis=1),
            pl.BlockSpec((1, d), lambda j, m: (0, 0)),
            pl.BlockSpec((d, tn), lambda j, m: (0, j), pipeline_mode=pl.Buffered(1)),
        ],
        out_specs=pl.BlockSpec((tm, tn), lambda j, m: (m, j)),
        compiler_params=_params("parallel", "parallel"),
        name="mixer_in_proj",
    )(x, mod3, mod3, gain.reshape(1, d), w)


def _gates(z, one_minus_lb):
    t = jnp.exp2(jnp.abs(z) * (-LOG2_E))
    r = 1.0 / (1.0 + t)
    k = one_minus_lb * jnp.where(z >= 0, t * r, r)
    log2f = jnp.log2(jnp.maximum(1.0 - k, MIN_FORGET))
    return log2f, k


def _hg_levels(c, leaf):
    widths = []
    w = leaf
    while w < c:
        widths.append(w)
        w *= 2
    return widths


def _hg_wide_constants(c, leaf, reverse):
    t = np.arange(c)[:, None]
    s = np.arange(c)[None, :]
    if reverse:
        t, s = c - 1 - t, c - 1 - s
    cum = (s <= t).astype(np.float32)
    level = np.full((c, c), -1, np.int32)
    level[(s <= t) & (t // leaf == s // leaf)] = 0
    for i, w in enumerate(_hg_levels(c, leaf)):
        level[(s <= t) & (t // (2 * w) == s // (2 * w)) & (t // w != s // w)] = i + 1
    n_levels = 1 + len(_hg_levels(c, leaf))
    wide = np.full((c, n_levels * c), -1, np.int32)
    for l in range(n_levels):
        wide[:, l * c:(l + 1) * c] = np.where(level == l, l, -1)
    return cum, wide, n_levels


def _hg_reference_rows(e_cum, leaf, reverse):
    c, width = e_cum.shape
    mid = leaf // 2 if reverse else leaf // 2 - 1
    refs = [jnp.concatenate(
        [jnp.broadcast_to(e_cum[b + mid:b + mid + 1, :], (leaf, width)) for b in range(0, c, leaf)], axis=0)]
    for w in _hg_levels(c, leaf):
        row = w if reverse else w - 1
        refs.append(jnp.concatenate(
            [jnp.broadcast_to(e_cum[p + row:p + row + 1, :], (2 * w, width)) for p in range(0, c, 2 * w)],
            axis=0))
    return refs


def _hgrn_wide_body(lay, nb, n_levels, n_g, reverse, with_cast, *refs):
    if with_cast:
        *refs, o_ref, w16_ref, state_scr = refs
        *refs, w32_ref = refs
        w16_ref[...] = w32_ref[...].astype(BF16)
    else:
        *refs, o_ref, state_scr = refs
    if reverse:
        q_ref, z_ref, v_ref, g_ref, fwd_ref, lb_ref, gain_ref, cum_ref, wl_ref = refs
    else:
        q_ref, z_ref, v_ref, lb_ref, cum_ref, wl_ref = refs
    step = pl.program_id(1)
    c = HG_CHUNK
    n_chunks = q_ref.shape[0] // c
    width = n_g * HEAD_DIM
    blk = (nb - 1 - step) if reverse else step
    row0 = blk * (n_chunks * c)
    bounds = [st + ln for st, ln in zip(lay.starts, lay.lens)] if reverse else list(lay.starts)
    edge = row0 + n_chunks * c if reverse else row0
    reset = functools.reduce(jnp.logical_or, [edge == b for b in bounds])

    @pl.when(reset)
    def _():
        state_scr[...] = jnp.zeros_like(state_scr)

    nt = (((1,), (1,)), ((), ()))
    tn = (((0,), (0,)), ((), ()))
    last_row = 0 if reverse else c - 1

    def chunk(j, carry):
        jj = (n_chunks - 1 - j) if reverse else j
        rows = pl.ds(pl.multiple_of(jj * c, c), c)
        q = q_ref[rows, :].astype(F32)
        v16 = v_ref[rows, :].astype(BF16)
        states = [state_scr[h] for h in range(n_g)]
        log2f, k = _gates(z_ref[rows, :].astype(F32), 1.0 - lb_ref[...])
        hi = log2f.astype(BF16)
        lo = (log2f - hi.astype(F32)).astype(BF16)
        e2 = jnp.dot(cum_ref[...], jnp.concatenate([hi, lo], axis=1), preferred_element_type=F32)
        e_cum = e2[:, :width] + e2[:, width:]
        e_last = e_cum[last_row:last_row + 1, :]
        refs_l = _hg_reference_rows(e_cum, HG_LEAF, reverse)
        e_leaf = e_cum - refs_l[0]
        q_parts = [q * jnp.exp2(e_leaf)]
        k_parts = [k * jnp.exp2(-e_leaf)]
        for ref in refs_l[1:]:
            delta = lax.bitcast_convert_type(e_cum - ref, jnp.uint32) | jnp.uint32(0x80000000)
            x = jnp.exp2(lax.bitcast_convert_type(delta, F32))
            q_parts.append(q * x)
            k_parts.append(k * x)
        q_in = (q * jnp.exp2(e_cum)).astype(BF16)
        k_out = (k * jnp.exp2(e_last - e_cum)).astype(BF16)
        state_decay = jnp.exp2(e_last)
        wide_level = wl_ref[...]
        lane_blocks = [slice(b, b + V7X_LANES) for b in range(0, n_levels * c, V7X_LANES)]
        masks = [[(wide_level[:, lb_] == l) for l in range(lb_.start // c, lb_.stop // c)] for lb_ in lane_blocks]
        outs = []
        for h in range(n_g):
            cols = slice(h * HEAD_DIM, (h + 1) * HEAD_DIM)
            q_stack = jnp.concatenate([p[:, cols] for p in q_parts], axis=0).astype(BF16)
            k_stack = jnp.concatenate([p[:, cols] for p in k_parts], axis=0).astype(BF16)
            r = lax.dot_general(q_stack, k_stack, nt, preferred_element_type=F32)
            pieces = []
            for lb_, lane_masks in zip(lane_blocks, masks):
                piece = jnp.zeros((c, V7X_LANES), F32)
                for l, mask in zip(range(lb_.start // c, lb_.stop // c), lane_masks):
                    piece = jnp.where(mask, r[l * c:(l + 1) * c, lb_], piece)
                pieces.append(piece)
            scores = jnp.concatenate(pieces, axis=1)
            v_h = v16[:, cols]
            o = jnp.dot(scores.astype(BF16), jnp.concatenate([v_h] * n_levels, axis=0),
                        preferred_element_type=F32)
            o = o + lax.dot_general(q_in[:, cols], states[h].astype(BF16), nt, preferred_element_type=F32)
            upd = lax.dot_general(v_h, k_out[:, cols], tn, preferred_element_type=F32)
            states[h] = states[h] * state_decay[:, cols] + upd
            outs.append(o)
        for h in range(n_g):
            state_scr[h] = states[h]
        if reverse:
            g = g_ref[rows, :].astype(F32)
            fwd = fwd_ref[rows, :]
            for h in range(n_g):
                cols = slice(h * HEAD_DIM, (h + 1) * HEAD_DIM)
                tot = fwd[:, cols] + outs[h]
                ms = jnp.mean(tot * tot, axis=-1, keepdims=True)
                y = tot * lax.rsqrt(ms + EPS) * gain_ref[...]
                o_ref[rows, cols] = (y * _silu(g[:, cols])).astype(o_ref.dtype)
        else:
            for h in range(n_g):
                o_ref[rows, h * HEAD_DIM:(h + 1) * HEAD_DIM] = outs[h]
        return carry

    lax.fori_loop(0, n_chunks, chunk, 0, unroll=HG_UNROLL)


def _hgrn_dir(proj, z_group, lb, fwd_out, out_gain, lay, n_heads, reverse, out_dtype, cast=None):
    t = proj.shape[0]
    lb_rows = _tile(int(np.gcd.reduce(lay.lens)), HG_BLOCK, HG_CHUNK)
    nb = t // lb_rows
    n_g = min(n_heads, HG_HEADS_PER_STEP)
    assert n_heads % n_g == 0
    n_groups = n_heads // n_g
    mats, level, n_levels = _hg_wide_constants(HG_CHUNK, HG_LEAF, reverse)
    width = n_g * HEAD_DIM

    def col(group):
        return lambda g, s: ((nb - 1 - s) if reverse else s, group * n_groups + g)

    blk = (lb_rows, width)
    const = lambda g, s: (0, 0)
    in_specs = [pl.BlockSpec(blk, col(0)), pl.BlockSpec(blk, col(z_group)), pl.BlockSpec(blk, col(3))]
    args = [proj, proj, proj]
    if reverse:
        in_specs += [pl.BlockSpec(blk, col(4)), pl.BlockSpec(blk, col(0))]
        args += [proj, fwd_out]
    in_specs.append(pl.BlockSpec((1, width), lambda g, s: (0, g)))
    args.append(lb.reshape(1, n_heads * HEAD_DIM))
    if reverse:
        in_specs.append(pl.BlockSpec((1, HEAD_DIM), const))
        args.append(out_gain.reshape(1, HEAD_DIM))
    in_specs += [pl.BlockSpec(mats.shape, const), pl.BlockSpec(level.shape, const)]
    args += [jnp.asarray(mats, BF16), jnp.asarray(level)]
    out_specs = pl.BlockSpec(blk, col(0))
    out_shape = jax.ShapeDtypeStruct((t, n_heads * HEAD_DIM), out_dtype)
    if cast is not None:
        src, first_row, n_rows, col_block, n_cols = cast
        assert n_groups == 1 and n_rows % (16 * nb) == 0 and first_row % (n_rows // nb) == 0
        chunk = n_rows // nb
        in_specs.append(pl.BlockSpec((chunk, n_cols), lambda g, s: (first_row // chunk + s, col_block)))
        args.append(src)
        out_specs = (out_specs, pl.BlockSpec((chunk, n_cols), lambda g, s: (s, 0)))
        out_shape = (out_shape, jax.ShapeDtypeStruct((n_rows, n_cols), BF16))
    return pl.pallas_call(
        functools.partial(_hgrn_wide_body, lay, nb, n_levels, n_g, reverse, cast is not None),
        out_shape=out_shape,
        grid=(n_groups, nb),
        in_specs=in_specs,
        out_specs=out_specs,
        scratch_shapes=[pltpu.VMEM((n_g, HEAD_DIM, HEAD_DIM), F32)],
        compiler_params=_params("parallel", "arbitrary"),
        name="hgrn2_bwd" if reverse else "hgrn2_fwd",
    )(*args)


def _hgrn(proj, lbs_f, lbs_b, out_gain, lay, n_heads, out_dtype, cast_fwd=None, cast_bwd=None):
    fwd = _hgrn_dir(proj, 1, lbs_f, None, None, lay, n_heads, False, F32, cast_fwd)
    o_fwd, w_a = fwd if cast_fwd is not None else (fwd, None)
    bwd = _hgrn_dir(proj, 2, lbs_b, o_fwd, out_gain, lay, n_heads, True, out_dtype, cast_bwd)
    o, w_b = bwd if cast_bwd is not None else (bwd, None)
    return o, w_a, w_b


def _rope_tables(n_pos):
    n_freq = HEAD_DIM // 4
    pos = jnp.arange(n_pos)
    coords = jnp.stack([pos // GRID_W, pos % GRID_W], axis=-1).astype(F32)
    inv = ROPE_THETA ** (-jnp.arange(n_freq, dtype=F32) / n_freq)
    ang = coords[:, :, None] * inv
    cos = jnp.broadcast_to(jnp.cos(ang)[:, :, None, :], (n_pos, 2, 2, n_freq))
    sin = jnp.sin(ang)
    sin = jnp.stack([-sin, sin], axis=2)
    return cos.reshape(n_pos, HEAD_DIM), sin.reshape(n_pos, HEAD_DIM)


def _qk_prep_body(n_q, n_kv, q_ref, k_ref, v_ref, cos_ref, sin_ref, qg_ref, kg_ref, swap_ref,
                  qo_ref, ko_ref, vo_ref):
    cos = cos_ref[...]
    sin = sin_ref[...]
    swap = swap_ref[...]

    def norm_rope(x, gain, scale):
        ms = jnp.mean(x * x, axis=-1, keepdims=True)
        y = x * lax.rsqrt(ms + EPS) * gain
        partner = jnp.dot(y.astype(BF16), swap, preferred_element_type=F32)
        out = y * cos + partner * sin
        return out * scale if scale is not None else out

    for h in range(n_q):
        cols = slice(h * HEAD_DIM, (h + 1) * HEAD_DIM)
        qo_ref[:, cols] = norm_rope(q_ref[:, cols].astype(F32), qg_ref[...],
                                    HEAD_DIM ** -0.5 * LOG2_E).astype(qo_ref.dtype)
    for h in range(n_kv):
        cols = slice(h * HEAD_DIM, (h + 1) * HEAD_DIM)
        ko_ref[:, cols] = norm_rope(k_ref[:, cols].astype(F32), kg_ref[...], None).astype(ko_ref.dtype)
    vo_ref[...] = v_ref[...].astype(vo_ref.dtype)


def _qk_prep(proj, q_gain, k_gain, cos, sin, lay, q_col, n_q, n_kv):
    t = proj.shape[0]
    tm = _row_tile(lay, 256)
    qw, kw = n_q * HEAD_DIM, n_kv * HEAD_DIM
    assert q_col % qw == 0 and (q_col + qw) % kw == 0

    def pos_block(m):
        row = m * tm
        return ((row - _seq_start(row, lay)) // tm, 0)

    quarter = HEAD_DIM // 4
    lane = np.arange(HEAD_DIM)
    source = np.where(lane % (2 * quarter) < quarter, lane + quarter, lane - quarter)
    swap = (lane[:, None] == source[None, :]).astype(np.float32)

    return pl.pallas_call(
        functools.partial(_qk_prep_body, n_q, n_kv),
        out_shape=(jax.ShapeDtypeStruct((t, qw), BF16), jax.ShapeDtypeStruct((t, kw), BF16),
                   jax.ShapeDtypeStruct((t, kw), BF16)),
        grid=(t // tm,),
        in_specs=[
            pl.BlockSpec((tm, qw), lambda m: (m, q_col // qw)),
            pl.BlockSpec((tm, kw), lambda m: (m, (q_col + qw) // kw)),
            pl.BlockSpec((tm, kw), lambda m: (m, (q_col + qw) // kw + 1)),
            pl.BlockSpec((tm, HEAD_DIM), pos_block),
            pl.BlockSpec((tm, HEAD_DIM), pos_block),
            pl.BlockSpec((1, HEAD_DIM), lambda m: (0, 0)),
            pl.BlockSpec((1, HEAD_DIM), lambda m: (0, 0)),
            pl.BlockSpec((HEAD_DIM, HEAD_DIM), lambda m: (0, 0)),
        ],
        out_specs=(pl.BlockSpec((tm, qw), lambda m: (m, 0)), pl.BlockSpec((tm, kw), lambda m: (m, 0)),
                   pl.BlockSpec((tm, kw), lambda m: (m, 0))),
        compiler_params=_params("parallel"),
        name="attn_qk_prep",
    )(proj, proj, proj, cos, sin, q_gain.reshape(1, HEAD_DIM), k_gain.reshape(1, HEAD_DIM),
      jnp.asarray(swap, BF16))


def _flash_body(with_cast, qb_ref, kb_ref, flag_ref, q_ref, k_ref, v_ref, *rest):
    if with_cast:
        w32_ref, o_ref, w16_ref, m_scr, l_scr, acc_scr = rest
        w16_ref[...] = w32_ref[...].astype(BF16)
    else:
        o_ref, m_scr, l_scr, acc_scr = rest
    i = pl.program_id(1)
    flags = flag_ref[i]
    tq = q_ref.shape[0] // FLASH_Q_SUBBLOCKS
    mrows = ATT_GROUP * tq

    @pl.when((flags & 1) != 0)
    def _():
        m_scr[...] = jnp.full_like(m_scr, -jnp.inf)
        l_scr[...] = jnp.zeros_like(l_scr)
        acc_scr[...] = jnp.zeros_like(acc_scr)

    k = k_ref[...]
    v = v_ref[...]
    for u in range(FLASH_Q_SUBBLOCKS):
        q = q_ref[u * tq:(u + 1) * tq, :]
        rows = slice(u * mrows, (u + 1) * mrows)
        qs = jnp.concatenate([q[:, g * HEAD_DIM:(g + 1) * HEAD_DIM] for g in range(ATT_GROUP)], axis=0)
        s = lax.dot_general(qs, k, (((1,), (1,)), ((), ())), preferred_element_type=F32)
        m_prev = m_scr[rows, :]
        m_new = jnp.maximum(m_prev, jnp.max(s, axis=-1, keepdims=True))
        alpha = jnp.exp2(m_prev - m_new)
        p = jnp.exp2(s - m_new)
        l_scr[rows, :] = alpha * l_scr[rows, :] + jnp.sum(p, axis=-1, keepdims=True)
        acc_scr[rows, :] = alpha * acc_scr[rows, :] + jnp.dot(p.astype(BF16), v, preferred_element_type=F32)
        m_scr[rows, :] = m_new

    @pl.when((flags & 2) != 0)
    def _():
        out = acc_scr[...] / l_scr[...]
        for u in range(FLASH_Q_SUBBLOCKS):
            for g in range(ATT_GROUP):
                o_ref[u * tq:(u + 1) * tq, g * HEAD_DIM:(g + 1) * HEAD_DIM] = (
                    out[u * mrows + g * tq:u * mrows + (g + 1) * tq].astype(o_ref.dtype))


def _flash_attention(q_rot, k_rot, v16, lay, n_kv, out_dtype, cast_src=None, cast_unit=0, cast_units=1):
    t = q_rot.shape[0]
    g = int(np.gcd.reduce(lay.lens))
    tq = FLASH_Q_SUBBLOCKS * _tile(g // FLASH_Q_SUBBLOCKS, 256, 16)
    tk = _tile(g, FLASH_KV_ROWS, V7X_LANES)
    qb, kb, flags = [], [], []
    for st, ln in zip(lay.starts, lay.lens):
        for qi in range(ln // tq):
            for ki in range(ln // tk):
                qb.append(st // tq + qi)
                kb.append(st // tk + ki)
                flags.append((1 if ki == 0 else 0) | (2 if ki == ln // tk - 1 else 0))
    n_items = len(qb)
    gw = ATT_GROUP * HEAD_DIM
    in_specs = [
        pl.BlockSpec((tq, gw), lambda h, i, qb, kb, fl: (qb[i], h)),
        pl.BlockSpec((tk, HEAD_DIM), lambda h, i, qb, kb, fl: (kb[i], h)),
        pl.BlockSpec((tk, HEAD_DIM), lambda h, i, qb, kb, fl: (kb[i], h)),
    ]
    out_specs = pl.BlockSpec((tq, gw), lambda h, i, qb, kb, fl: (qb[i], h))
    out_shape = jax.ShapeDtypeStruct((t, n_kv * gw), out_dtype)
    args = [q_rot, k_rot, v16]
    if cast_src is not None:
        rows, cols = cast_src.shape[0] // cast_units, cast_src.shape[1]
        n_steps = n_kv * n_items
        chunk = min(c for c in range(16, rows + 1, 16) if rows % c == 0 and rows // c <= n_steps)
        n_chunks = rows // chunk

        def chunk_of(h, i):
            return jnp.minimum(h * n_items + i, n_chunks - 1)

        in_specs.append(pl.BlockSpec((chunk, cols),
                                     lambda h, i, qb, kb, fl: (cast_unit * n_chunks + chunk_of(h, i), 0)))
        out_specs = (out_specs, pl.BlockSpec((chunk, cols), lambda h, i, qb, kb, fl: (chunk_of(h, i), 0)))
        out_shape = (out_shape, jax.ShapeDtypeStruct((rows, cols), BF16))
        args.append(cast_src)
    grid_spec = pltpu.PrefetchScalarGridSpec(
        num_scalar_prefetch=3,
        grid=(n_kv, n_items),
        in_specs=in_specs,
        out_specs=out_specs,
        scratch_shapes=[pltpu.VMEM((ATT_GROUP * tq, 1), F32), pltpu.VMEM((ATT_GROUP * tq, 1), F32),
                        pltpu.VMEM((ATT_GROUP * tq, HEAD_DIM), F32)],
    )
    return pl.pallas_call(
        functools.partial(_flash_body, cast_src is not None),
        out_shape=out_shape,
        grid_spec=grid_spec,
        compiler_params=_params("arbitrary", "arbitrary"),
        name="attn_flash",
    )(jnp.asarray(qb, jnp.int32), jnp.asarray(kb, jnp.int32), jnp.asarray(flags, jnp.int32), *args)


def _out_proj_body(a_ref, b_ref, wa_ref, wb_ref, x_ref, g_ref, o_ref):
    acc = jnp.dot(a_ref[...], wa_ref[...], preferred_element_type=F32)
    acc = acc + jnp.dot(b_ref[...], wb_ref[...], preferred_element_type=F32)
    o_ref[...] = x_ref[...] + g_ref[0] * acc


def _out_proj(o_hg, o_att, w_out, x, mod3, lay, layer):
    t, d = x.shape
    ka, kb = o_hg.shape[1], o_att.shape[1]
    tm = _row_tile(lay, 512)
    return pl.pallas_call(
        _out_proj_body,
        out_shape=jax.ShapeDtypeStruct((t, d), F32),
        grid=(t // tm,),
        in_specs=[
            pl.BlockSpec((tm, ka), lambda m: (m, 0)),
            pl.BlockSpec((tm, kb), lambda m: (m, 0)),
            pl.BlockSpec((ka, d), lambda m: (0, 0)),
            pl.BlockSpec((kb, d), lambda m: (1, 0)),
            pl.BlockSpec((tm, d), lambda m: (m, 0)),
            _mod_spec(lay, tm, layer, 2, d),
        ],
        out_specs=pl.BlockSpec((tm, d), lambda m: (m, 0)),
        compiler_params=_params("parallel"),
        name="mixer_out_proj",
    )(o_hg, o_att, w_out, w_out, x, mod3)


def _router_body(x_ref, sh_ref, sc_ref, gain_ref, wr_ref, o_ref, idx_ref, gate_ref):
    h = _norm_modulate(x_ref[...], gain_ref[...], sc_ref[0], sh_ref[0])
    o_ref[...] = h.astype(o_ref.dtype)
    wr = wr_ref[...]
    h_hi, w_hi = h.astype(BF16), wr.astype(BF16)
    h_lo = (h - h_hi.astype(F32)).astype(BF16)
    w_lo = (wr - w_hi.astype(F32)).astype(BF16)
    logits = (jnp.dot(h_hi, w_hi, preferred_element_type=F32) + jnp.dot(h_lo, w_hi, preferred_element_type=F32)
              + jnp.dot(h_hi, w_lo, preferred_element_type=F32))
    n_e = logits.shape[1]
    lane = lax.broadcasted_iota(jnp.int32, logits.shape, 1)
    m1 = jnp.max(logits, axis=-1, keepdims=True)
    i1 = jnp.min(jnp.where(logits == m1, lane, n_e), axis=-1, keepdims=True)
    rest = jnp.where(lane == i1, -jnp.inf, logits)
    m2 = jnp.max(rest, axis=-1, keepdims=True)
    i2 = jnp.min(jnp.where(rest == m2, lane, n_e), axis=-1, keepdims=True)
    e = jnp.exp(m2 - m1)
    tot = 1.0 + e
    two = lax.broadcasted_iota(jnp.int32, idx_ref.shape, 1)
    idx_ref[...] = jnp.where(two == 0, i1, i2)
    gate_ref[...] = jnp.where(two == 0, 1.0 / tot, e / tot)


def _router(x, mod3, gain, lay, layer, w_router):
    t, d = x.shape
    tm = _row_tile(lay, 256)
    in_specs = [
        pl.BlockSpec((tm, d), lambda m: (m, 0)),
        _mod_spec(lay, tm, layer, 3, d),
        _mod_spec(lay, tm, layer, 4, d),
        pl.BlockSpec((1, d), lambda m: (0, 0)),
    ]
    row_spec = pl.BlockSpec((tm, d), lambda m: (m, 0))
    n_e = w_router.shape[1]
    small = pl.BlockSpec((tm, TOP_K), lambda m: (m, 0))
    return pl.pallas_call(
        _router_body,
        out_shape=(jax.ShapeDtypeStruct((t, d), F32), jax.ShapeDtypeStruct((t, TOP_K), jnp.int32),
                   jax.ShapeDtypeStruct((t, TOP_K), F32)),
        grid=(t // tm,),
        in_specs=in_specs + [pl.BlockSpec((d, n_e), lambda m: (0, 0))],
        out_specs=(row_spec, small, small),
        compiler_params=_params("parallel"), name="moe_router",
    )(x, mod3, mod3, gain.reshape(1, d), w_router)


def _swiglu_step(x16, wg_ref, wu_ref, wd_refs, acc_scr):
    gate = jnp.dot(x16, wg_ref[0], preferred_element_type=F32)
    up = jnp.dot(x16, wu_ref[0], preferred_element_type=F32)
    act = (_silu(gate) * up).astype(BF16)
    col = 0
    for wd_ref in wd_refs:
        n = wd_ref.shape[2]
        acc_scr[:, col:col + n] += jnp.dot(act, wd_ref[0], preferred_element_type=F32)
        col += n


def _dense_ffn_body(x_ref, sh_ref, sc_ref, gain_ref, wg_ref, wu_ref, wd_ref, g_ref, o_ref, x16_scr, acc_scr):
    f = pl.program_id(1)

    @pl.when(f == 0)
    def _():
        x16_scr[...] = _norm_modulate(x_ref[...], gain_ref[...], sc_ref[0], sh_ref[0]).astype(BF16)
        acc_scr[...] = jnp.zeros_like(acc_scr)

    _swiglu_step(x16_scr[...], wg_ref, wu_ref, (wd_ref,), acc_scr)

    @pl.when(f == pl.num_programs(1) - 1)
    def _():
        o_ref[...] = x_ref[...] + g_ref[0] * acc_scr[...]


def _dense_ffn(x, mod3, gain, w_gate, w_up, w_down, j, lay, layer):
    t, d = x.shape
    fdim = w_gate.shape[2]
    tm = _row_tile(lay, 512)
    tf = _tile(fdim, FFN_COLS, V7X_LANES)
    return pl.pallas_call(
        _dense_ffn_body,
        out_shape=jax.ShapeDtypeStruct((t, d), F32),
        grid=(t // tm, fdim // tf),
        in_specs=[
            pl.BlockSpec((tm, d), lambda m, f: (m, 0)),
            _mod_spec(lay, tm, layer, 3, d),
            _mod_spec(lay, tm, layer, 4, d),
            pl.BlockSpec((1, d), lambda m, f: (0, 0)),
            pl.BlockSpec((1, d, tf), lambda m, f: (j, 0, f)),
            pl.BlockSpec((1, d, tf), lambda m, f: (j, 0, f)),
            pl.BlockSpec((1, tf, d), lambda m, f: (j, f, 0)),
            _mod_spec(lay, tm, layer, 5, d),
        ],
        out_specs=pl.BlockSpec((tm, d), lambda m, f: (m, 0)),
        scratch_shapes=[pltpu.VMEM((tm, d), BF16), pltpu.VMEM((tm, d), F32)],
        compiler_params=_params("parallel", "arbitrary"),
        name="swiglu_residual",
    )(x, mod3, mod3, gain.reshape(1, d), w_gate, w_up, w_down, mod3)


def _expert_ffn_body(n_f, n_down, be_ref, nu_ref, tok_ref, h_hbm, wg_ref, wu_ref, *rest):
    wd_refs = rest[:n_down]
    o_ref, xbuf, x16_scr, acc_scr, sems = rest[n_down:]
    m = pl.program_id(0)
    f = pl.program_id(1)
    tm = x16_scr.shape[0]
    n_used = nu_ref[0]
    used = m < n_used
    last = f == n_f - 1

    def row_copy(src_row, slot, dst_row):
        return pltpu.make_async_copy(h_hbm.at[pl.ds(src_row, 1)], xbuf.at[slot, pl.ds(dst_row, 1)],
                                     sems.at[slot])

    slot = m % 2
    share = tm // n_f
    extra = tm - share * n_f

    @pl.when(jnp.logical_and(used, f == 0))
    def _():
        @pl.when(m == 0)
        def _():
            def issue(r, carry):
                row_copy(tok_ref[r], 0, r).start()
                return carry
            lax.fori_loop(0, tm, issue, 0)

        for _ in range(tm):
            row_copy(0, slot, 0).wait()
        x16_scr[...] = xbuf[slot].astype(BF16)
        acc_scr[...] = jnp.zeros_like(acc_scr)

    first = (m + 1) * tm

    @pl.when(used)
    def _():
        for r in range(share):
            row = f * share + r
            row_copy(tok_ref[first + row], 1 - slot, row).start()
        _swiglu_step(x16_scr[...], wg_ref, wu_ref, wd_refs, acc_scr)

    @pl.when(jnp.logical_and(used, last))
    def _():
        for r in range(share * n_f, share * n_f + extra):
            row_copy(tok_ref[first + r], 1 - slot, r).start()
        o_ref[...] = acc_scr[...]

        @pl.when(m + 1 == n_used)
        def _():
            for _ in range(tm):
                row_copy(0, 1 - slot, 0).wait()

    @pl.when(jnp.logical_and(jnp.logical_not(used), last))
    def _():
        o_ref[...] = jnp.zeros_like(o_ref)


def _expert_ffn(h32, row_token, w_gate, w_up, w_down_slabs, block_expert, n_used, tm):
    d = h32.shape[1]
    r = row_token.shape[0]
    fdim = w_gate.shape[2]
    tf = _tile(fdim, FFN_COLS, V7X_LANES)
    nf = fdim // tf

    def fcol(m, f, nu):
        return jnp.where(m < nu[0], f, nf - 1)

    grid_spec = pltpu.PrefetchScalarGridSpec(
        num_scalar_prefetch=3,
        grid=(r // tm, nf),
        in_specs=[
            pl.BlockSpec(memory_space=pl.ANY),
            pl.BlockSpec((1, d, tf), lambda m, f, be, nu, tok: (be[m], 0, fcol(m, f, nu))),
            pl.BlockSpec((1, d, tf), lambda m, f, be, nu, tok: (be[m], 0, fcol(m, f, nu))),
        ] + [pl.BlockSpec((1, tf, w.shape[2]), lambda m, f, be, nu, tok: (be[m], fcol(m, f, nu), 0))
             for w in w_down_slabs],
        out_specs=pl.BlockSpec((tm, d), lambda m, f, be, nu, tok: (m, 0)),
        scratch_shapes=[pltpu.VMEM((2, tm, d), F32), pltpu.VMEM((tm, d), BF16), pltpu.VMEM((tm, d), F32),
                        pltpu.SemaphoreType.DMA((2,))],
    )
    return pl.pallas_call(
        functools.partial(_expert_ffn_body, nf, len(w_down_slabs)),
        out_shape=jax.ShapeDtypeStruct((r, d), F32),
        grid_spec=grid_spec,
        compiler_params=_params("arbitrary", "arbitrary"),
        name="swiglu_experts",
    )(block_expert, n_used, row_token, h32, w_gate, w_up, *w_down_slabs)


def _combine_body(dest_ref, yb_hbm, x_ref, gate_ref, g_ref, o_ref, buf, sems):
    tm = x_ref.shape[0]
    i = pl.program_id(0)
    slot = i % 2

    def row_copy(src_row, slot_, k, dst_row):
        return pltpu.make_async_copy(yb_hbm.at[pl.ds(src_row, 1)], buf.at[slot_, k, pl.ds(dst_row, 1)],
                                     sems.at[slot_])

    def gather(tile, slot_):
        def issue(r, carry):
            for k in range(TOP_K):
                row_copy(dest_ref[(tile * tm + r) * TOP_K + k], slot_, k, r).start()
            return carry
        lax.fori_loop(0, tm, issue, 0)

    @pl.when(i == 0)
    def _():
        gather(0, 0)

    @pl.when(i + 1 < pl.num_programs(0))
    def _():
        gather(i + 1, 1 - slot)

    for _ in range(tm * TOP_K):
        row_copy(0, slot, 0, 0).wait()
    gates = gate_ref[...]
    y = buf[slot, 0] * gates[:, 0:1]
    for k in range(1, TOP_K):
        y = y + buf[slot, k] * gates[:, k:k + 1]
    o_ref[...] = x_ref[...] + g_ref[0] * y


def _combine(yb, dest, gates, x, mod3, lay, layer):
    t, d = x.shape
    tm = _row_tile(lay, 256)
    grid_spec = pltpu.PrefetchScalarGridSpec(
        num_scalar_prefetch=1,
        grid=(t // tm,),
        in_specs=[
            pl.BlockSpec(memory_space=pl.ANY),
            pl.BlockSpec((tm, d), lambda m, dest: (m, 0)),
            pl.BlockSpec((tm, TOP_K), lambda m, dest: (m, 0)),
            _mod_spec(lay, tm, layer, 5, d),
        ],
        out_specs=pl.BlockSpec((tm, d), lambda m, dest: (m, 0)),
        scratch_shapes=[pltpu.VMEM((2, TOP_K, tm, d), yb.dtype), pltpu.SemaphoreType.DMA((2,))],
    )
    return pl.pallas_call(
        _combine_body,
        out_shape=jax.ShapeDtypeStruct((t, d), F32),
        grid_spec=grid_spec,
        compiler_params=_params("arbitrary"),
        name="moe_combine",
    )(dest, yb, x, gates, mod3)


def _moe(x, h32, idx, gates, w_gate, w_up, w_down_slabs, n_e, mod3, lay, layer):
    t, d = x.shape
    tm = _row_tile(lay, 512)
    n_slots = t * TOP_K
    cap = n_slots + n_e * tm
    e = idx.reshape(-1)
    onehot = (e[:, None] == jnp.arange(n_e, dtype=jnp.int32)[None, :]).astype(jnp.int32)
    csum = jnp.cumsum(onehot, axis=0)
    rank = jnp.take_along_axis(csum, e[:, None], axis=1)[:, 0] - 1
    counts = csum[-1]
    padded = (counts + tm - 1) // tm * tm
    pad_end = jnp.cumsum(padded)
    pad_start = pad_end - padded
    dest = (pad_start[e] + rank).astype(jnp.int32)
    row_token = jnp.zeros((cap,), jnp.int32).at[dest].set(jnp.arange(n_slots, dtype=jnp.int32) // TOP_K)
    block_first = jnp.arange(cap // tm, dtype=jnp.int32) * tm
    block_e = jnp.minimum(jnp.searchsorted(pad_end, block_first, side='right'), n_e - 1).astype(jnp.int32)
    n_used = (pad_end[-1:] // tm).astype(jnp.int32)
    block_e = jnp.where(block_first < pad_end[-1], block_e, block_e[jnp.maximum(n_used[0] - 1, 0)])
    yb = _expert_ffn(h32, row_token, w_gate, w_up, w_down_slabs, block_e, n_used, tm)
    return _combine(yb, dest, gates, x, mod3, lay, layer)


def _final_norm_body(x_ref, gain_ref, o_ref):
    x = x_ref[...]
    ms = jnp.mean(x * x, axis=-1, keepdims=True)
    o_ref[...] = x * lax.rsqrt(ms + EPS) * gain_ref[...]


def _final_norm(x, gain, row0, n_rows, tm):
    d = x.shape[1]
    first = row0 // tm
    return pl.pallas_call(
        _final_norm_body,
        out_shape=jax.ShapeDtypeStruct((n_rows, d), F32),
        grid=(n_rows // tm,),
        in_specs=[pl.BlockSpec((tm, d), lambda m: (first + m, 0)), pl.BlockSpec((1, d), lambda m: (0, 0))],
        out_specs=pl.BlockSpec((tm, d), lambda m: (m, 0)),
        compiler_params=_params("parallel"),
        name="final_norm",
    )(x, gain.reshape(1, d))


def kernel(x_prompt, x_sample, c_prompt, c_sample, w_mod, b_mod, norm_mix, norm_ffn, w_in, hg_lb_logits,
           hg_out_norm, qk_norm, w_out, ffn_gate, ffn_up, ffn_down, router, exp_gate, exp_up, exp_down,
           final_norm):
    depth, d = norm_mix.shape
    lay = _layout(x_prompt, x_sample)
    n_seq = len(lay.starts)
    assert n_seq <= MOD_ROWS
    hg_heads = hg_lb_logits.shape[2] // HEAD_DIM
    kv_heads = (w_in.shape[2] - 5 * hg_heads * HEAD_DIM) // HEAD_DIM // (ATT_GROUP + 2)
    att_heads = kv_heads * ATT_GROUP
    hg_width = hg_heads * HEAD_DIM
    n_e = exp_gate.shape[1]

    x = jnp.concatenate([x_prompt.reshape(-1, d), x_sample.reshape(-1, d)], axis=0)
    c_rows = jnp.concatenate([c_prompt, c_sample, jnp.zeros((MOD_ROWS - n_seq, d), F32)], axis=0)
    mod3 = _modulation(c_rows, w_mod, b_mod).reshape(depth * MOD_ROWS, 1, 6 * d)
    lbs = _lower_bounds(hg_lb_logits)
    cos, sin = _rope_tables(max(lay.lens))

    w_in16 = w_in.astype(BF16)
    w_out16 = w_out.astype(BF16)
    dense16 = [w.astype(BF16) for w in (ffn_gate, ffn_up, ffn_down)]
    n_moe = exp_gate.shape[0]
    assert depth == 2 * n_moe, "one attention call per (MoE layer, gate / up) weight slab"
    exp_gate2 = exp_gate.reshape(-1, exp_gate.shape[-1])
    exp_up2 = exp_up.reshape(-1, exp_up.shape[-1])
    exp_down2 = exp_down.reshape(-1, d)
    down_rows = n_e * exp_down.shape[2]
    expert16 = [[None] * n_moe, [None] * n_moe]
    down16 = [None] * n_moe

    for layer in range(depth):
        proj = _in_proj(x, mod3, norm_mix[layer], w_in16[layer], lay, layer, PROJ_DTYPE)
        if layer % 2 == 0:
            casts = [(exp_down2, (layer // 2) * down_rows, down_rows, half, d // 2) for half in range(2)]
            o_hg, *slabs = _hgrn(proj, lbs[0, layer], lbs[1, layer], hg_out_norm[layer], lay, hg_heads, BF16,
                                 *casts)
            down16[layer // 2] = [s.reshape(n_e, -1, d // 2) for s in slabs]
        else:
            o_hg, _, _ = _hgrn(proj, lbs[0, layer], lbs[1, layer], hg_out_norm[layer], lay, hg_heads, BF16)
        q_rot, k_rot, v16 = _qk_prep(proj, qk_norm[layer, 0], qk_norm[layer, 1], cos, sin, lay,
                                     5 * hg_width, att_heads, kv_heads)
        j = layer // 2
        slab = exp_gate2 if layer % 2 == 0 else exp_up2
        o_att, slab16 = _flash_attention(q_rot, k_rot, v16, lay, kv_heads, BF16, cast_src=slab,
                                         cast_unit=j, cast_units=n_moe)
        expert16[layer % 2][j] = slab16.reshape(n_e, d, -1)
        x = _out_proj(o_hg, o_att, w_out16[layer], x, mod3, lay, layer)
        if layer % 2 == 0:
            x = _dense_ffn(x, mod3, norm_ffn[layer], *dense16, j, lay, layer)
        else:
            h32, idx, gates = _router(x, mod3, norm_ffn[layer], lay, layer, router[j])
            x = _moe(x, h32, idx, gates, expert16[0][j], expert16[1][j], down16[j], n_e, mod3, lay, layer)

    tm = _row_tile(lay, 512)
    n_prompt = x_prompt.shape[0] * x_prompt.shape[1]
    y_prompt = _final_norm(x, final_norm, 0, n_prompt, tm).reshape(x_prompt.shape)
    y_sample = _final_norm(x, final_norm, n_prompt, lay.total - n_prompt, tm).reshape(x_sample.shape)
    return (y_prompt, y_sample)
```
